```python
import jax, jax.numpy as jnp
from jax import lax
import numpy as np

D_MODEL = 1024
BATCH = 4
SEQ = 8192
DEPTH = 1

POOL_WINDOWS = (2, 4, 8, 16)
POOL_WIDTH = D_MODEL // 2
POOL_GROUP = POOL_WIDTH // len(POOL_WINDOWS)
HEAD_DIM = 64
N_HEADS = (D_MODEL // 2) // HEAD_DIM
N_KV_HEADS = 2
ATTN_WIDTH = N_HEADS * HEAD_DIM
KV_WIDTH = N_KV_HEADS * HEAD_DIM
WINDOW = 128
BLOCK = 128
ROPE_THETA = 500000.0
ROT_DIM = HEAD_DIM // 4
MIX_WIDTH = POOL_WIDTH + ATTN_WIDTH
IN_WIDTH = POOL_WIDTH + ATTN_WIDTH + 2 * KV_WIDTH
D_FF = 2816
EPS = 1e-6
NEG_INF = -1e30

kernel_name = "hybrid_pool_swa_macaron_block"


def rms_norm(x, g):
    xf = x.astype(jnp.float32)
    y = xf * lax.rsqrt(jnp.mean(xf * xf, axis=-1, keepdims=True) + EPS)
    return (y * g.astype(jnp.float32)).astype(x.dtype)


def swiglu(h, w_gu, w_down):
    gate, up = jnp.split(h @ w_gu, 2, axis=-1)
    return (jax.nn.silu(gate) * up) @ w_down


def pool_mix(u, w_pool, pool_scale):
    B, S, C = u.shape
    uf = u.astype(jnp.float32)
    cs = jnp.concatenate([jnp.zeros((B, 1, C), jnp.float32), jnp.cumsum(uf, axis=1)], axis=1)
    t = jnp.arange(S)
    outs = []
    for g, w in enumerate(POOL_WINDOWS):
        lo, hi = g * POOL_GROUP, (g + 1) * POOL_GROUP
        start = jnp.maximum(t + 1 - w, 0)
        cnt = (t + 1 - start).astype(jnp.float32)
        csg = cs[:, :, lo:hi]
        mean = (csg[:, 1:] - csg[:, start]) / cnt[None, :, None]
        outs.append(mean - uf[:, :, lo:hi])
    d = jnp.stack(outs, axis=2).astype(u.dtype)
    y = jnp.einsum('bsgc,gcd->bsgd', d, w_pool).reshape(B, S, POOL_WIDTH)
    return y * pool_scale


def apply_partial_rope(x, cos, sin):
    half = ROT_DIM // 2
    x1, x2 = x[..., :half], x[..., half:ROT_DIM]
    rot = jnp.concatenate([x1 * cos - x2 * sin, x2 * cos + x1 * sin], axis=-1)
    return jnp.concatenate([rot, x[..., ROT_DIM:]], axis=-1)


def swa_with_sinks(q, k, v, sinks):
    B, S = q.shape[0], q.shape[1]
    nb = S // BLOCK
    G = N_HEADS // N_KV_HEADS
    qb = q.reshape(B, nb, BLOCK, N_KV_HEADS, G, HEAD_DIM)
    pad = ((0, 0), (BLOCK, 0), (0, 0), (0, 0))
    kp = jnp.pad(k, pad).reshape(B, nb + 1, BLOCK, N_KV_HEADS, HEAD_DIM)
    vp = jnp.pad(v, pad).reshape(B, nb + 1, BLOCK, N_KV_HEADS, HEAD_DIM)
    kb = jnp.concatenate([kp[:, :-1], kp[:, 1:]], axis=2)
    vb = jnp.concatenate([vp[:, :-1], vp[:, 1:]], axis=2)
    s = jnp.einsum('bnqkgd,bnjkd->bnkgqj', qb, kb,
                   preferred_element_type=jnp.float32) * (HEAD_DIM ** -0.5)
    qi = jnp.arange(BLOCK)[:, None]
    kj = jnp.arange(2 * BLOCK)[None, :]
    diff = qi + BLOCK - kj
    band = (diff >= 0) & (diff < WINDOW)
    key_abs = jnp.arange(nb)[:, None] * BLOCK - BLOCK + kj
    valid = band[None] & (key_abs >= 0)[:, None, :]
    s = jnp.where(valid[None, :, None, None], s, NEG_INF)
    sink = jnp.broadcast_to(sinks.astype(jnp.float32).reshape(1, 1, N_KV_HEADS, G, 1, 1),
                            s.shape[:-1] + (1,))
    p = jax.nn.softmax(jnp.concatenate([s, sink], axis=-1), axis=-1)[..., :-1]
    o = jnp.einsum('bnkgqj,bnjkd->bnqkgd', p.astype(v.dtype), vb)
    return o.reshape(B, S, ATTN_WIDTH)


def token_mixer(h, cos, sin, w_in, w_pool, pool_scale, sinks, g_pool, g_attn, w_out):
    B, S, _ = h.shape
    z = h @ w_in
    u = z[..., :POOL_WIDTH]
    q = z[..., POOL_WIDTH:MIX_WIDTH].reshape(B, S, N_HEADS, HEAD_DIM)
    k = z[..., MIX_WIDTH:MIX_WIDTH + KV_WIDTH].reshape(B, S, N_KV_HEADS, HEAD_DIM)
    v = z[..., MIX_WIDTH + KV_WIDTH:].reshape(B, S, N_KV_HEADS, HEAD_DIM)
    pool_out = pool_mix(u, w_pool, pool_scale)
    q = apply_partial_rope(q, cos, sin)
    k = apply_partial_rope(k, cos, sin)
    attn_out = swa_with_sinks(q, k, v, sinks)
    y = jnp.concatenate([rms_norm(pool_out, g_pool), rms_norm(attn_out, g_attn)], axis=-1)
    return y @ w_out


def setup_inputs(seed: int = 0) -> dict:
    key = jax.random.key(seed)
    ks = jax.random.split(key, 24)
    f32 = jnp.float32

    def w(k, shape, fan_in):
        return jax.random.normal(k, shape, f32) * fan_in ** -0.5

    def gain(k, n):
        return 1.0 + 0.05 * jax.random.normal(k, (DEPTH, n), f32)

    x = jax.random.normal(ks[0], (BATCH, SEQ, D_MODEL), f32)
    offset = jax.random.randint(ks[1], (BATCH, 1), 0, 4096, jnp.int32)
    positions = offset + jnp.arange(SEQ, dtype=jnp.int32)[None, :]
    return {
        "x": x,
        "positions": positions,
        "ffn1_pre": gain(ks[2], D_MODEL),
        "ffn1_w_gu": w(ks[3], (DEPTH, D_MODEL, 2 * D_FF), D_MODEL),
        "ffn1_w_down": w(ks[4], (DEPTH, D_FF, D_MODEL), D_FF),
        "ffn1_post": gain(ks[5], D_MODEL),
        "mix_pre": gain(ks[6], D_MODEL),
        "w_in": w(ks[7], (DEPTH, D_MODEL, IN_WIDTH), D_MODEL),
        "w_pool": w(ks[8], (DEPTH, len(POOL_WINDOWS), POOL_GROUP, POOL_GROUP), POOL_GROUP),
        "pool_scale": 0.5 + 0.05 * jax.random.normal(ks[9], (DEPTH, POOL_WIDTH), f32),
        "sinks": 0.5 * jax.random.normal(ks[10], (DEPTH, N_HEADS), f32),
        "g_pool": gain(ks[11], POOL_WIDTH),
        "g_attn": gain(ks[12], ATTN_WIDTH),
        "w_out": w(ks[13], (DEPTH, MIX_WIDTH, D_MODEL), MIX_WIDTH),
        "mix_post": gain(ks[14], D_MODEL),
        "ffn2_pre": gain(ks[15], D_MODEL),
        "ffn2_w_gu": w(ks[16], (DEPTH, D_MODEL, 2 * D_FF), D_MODEL),
        "ffn2_w_down": w(ks[17], (DEPTH, D_FF, D_MODEL), D_FF),
        "ffn2_post": gain(ks[18], D_MODEL),
    }


def reference(x, positions, ffn1_pre, ffn1_w_gu, ffn1_w_down, ffn1_post,
              mix_pre, w_in, w_pool, pool_scale, sinks, g_pool, g_attn, w_out, mix_post,
              ffn2_pre, ffn2_w_gu, ffn2_w_down, ffn2_post):
    inv_freq = ROPE_THETA ** (-jnp.arange(0, ROT_DIM, 2, dtype=jnp.float32) / ROT_DIM)
    ang = positions.astype(jnp.float32)[..., None] * inv_freq
    cos = jnp.cos(ang)[:, :, None, :].astype(x.dtype)
    sin = jnp.sin(ang)[:, :, None, :].astype(x.dtype)
    for l in range(DEPTH):
        h = swiglu(rms_norm(x, ffn1_pre[l]), ffn1_w_gu[l], ffn1_w_down[l])
        x = x + 0.5 * rms_norm(h, ffn1_post[l])
        h = token_mixer(rms_norm(x, mix_pre[l]), cos, sin, w_in[l], w_pool[l], pool_scale[l],
                        sinks[l], g_pool[l], g_attn[l], w_out[l])
        x = x + rms_norm(h, mix_post[l])
        h = swiglu(rms_norm(x, ffn2_pre[l]), ffn2_w_gu[l], ffn2_w_down[l])
        x = x + 0.5 * rms_norm(h, ffn2_post[l])
    return x
```

```python
import functools

import jax
import jax.numpy as jnp
from jax import lax
from jax.experimental import pallas as pl
from jax.experimental.pallas import tpu as pltpu

D_MODEL = 1024
D_FF = 2816
POOL_WINDOWS = (2, 4, 8, 16)
POOL_WIDTH = 512
POOL_GROUP = 128
HEAD_DIM = 64
N_HEADS = 8
N_KV_HEADS = 2
ATTN_WIDTH = 512
KV_WIDTH = 128
BLOCK = 128
ROPE_THETA = 500000.0
ROT_DIM = 16
MIX_WIDTH = 1024
IN_WIDTH = 1280
EPS = 1e-6
NEG_INF = -1e30

LANES = 128
POOL_HALO = 16

FFN_TOKENS = 512
FFN_CHUNK = 256
MIX_TOKENS = 512
FFN_VMEM_BYTES = 48 * 1024 * 1024
MIX_VMEM_BYTES = 48 * 1024 * 1024

_F32 = jnp.float32
_BF16 = jnp.bfloat16


def _rms(x, g):
    return x * lax.rsqrt(jnp.mean(x * x, axis=-1, keepdims=True) + EPS) * g


def _ffn_kernel(x_ref, pre_ref, wgu_ref, wd_ref, post_ref, o_ref, xn_ref, acc_ref):
    x = x_ref[...]
    xn_ref[...] = _rms(x, pre_ref[...]).astype(_BF16)
    n_chunks = wgu_ref.shape[1]

    def chunk(c, carry):
        xn = xn_ref[...]
        g = jnp.dot(xn, wgu_ref[0, c], preferred_element_type=_F32)
        u = jnp.dot(xn, wgu_ref[1, c], preferred_element_type=_F32)
        a = (g * jax.nn.sigmoid(g) * u).astype(_BF16)
        contrib = jnp.dot(a, wd_ref[c], preferred_element_type=_F32)

        @pl.when(c == 0)
        def _():
            acc_ref[...] = contrib

        @pl.when(c > 0)
        def _():
            acc_ref[...] += contrib

        return carry

    lax.fori_loop(0, n_chunks, chunk, 0)
    o_ref[...] = x + 0.5 * _rms(acc_ref[...], post_ref[...])


def _ffn(x2d, pre, w_gu, w_down, post):
    n_tok = x2d.shape[0]
    n_chunks = D_FF // FFN_CHUNK
    wgu = w_gu.astype(_BF16).reshape(D_MODEL, 2, n_chunks, FFN_CHUNK).transpose(1, 2, 0, 3)
    wd = w_down.astype(_BF16).reshape(n_chunks, FFN_CHUNK, D_MODEL)
    const = lambda nd: (lambda i: (0,) * nd)
    resident = dict(pipeline_mode=pl.Buffered(1))
    return pl.pallas_call(
        _ffn_kernel,
        grid=(n_tok // FFN_TOKENS,),
        in_specs=[
            pl.BlockSpec((FFN_TOKENS, D_MODEL), lambda i: (i, 0)),
            pl.BlockSpec((1, D_MODEL), const(2), **resident),
            pl.BlockSpec((2, n_chunks, D_MODEL, FFN_CHUNK), const(4), **resident),
            pl.BlockSpec((n_chunks, FFN_CHUNK, D_MODEL), const(3), **resident),
            pl.BlockSpec((1, D_MODEL), const(2), **resident),
        ],
        out_specs=pl.BlockSpec((FFN_TOKENS, D_MODEL), lambda i: (i, 0)),
        out_shape=jax.ShapeDtypeStruct(x2d.shape, _F32),
        scratch_shapes=[
            pltpu.VMEM((FFN_TOKENS, D_MODEL), _BF16),
            pltpu.VMEM((FFN_TOKENS, D_MODEL), _F32),
        ],
        compiler_params=pltpu.CompilerParams(
            dimension_semantics=("arbitrary",), vmem_limit_bytes=FFN_VMEM_BYTES),
    )(x2d, pre.reshape(1, D_MODEL), wgu, wd, post.reshape(1, D_MODEL))


def _rope(t, cos_t, sin_t, lo_lanes):
    partner = jnp.where(lo_lanes, pltpu.roll(t, LANES - ROT_DIM // 2, 1), pltpu.roll(t, ROT_DIM // 2, 1))
    return t * cos_t + partner * sin_t


def _mixer_kernel(sinks_ref, x_ref, pos_ref, invf_ref, pre_ref, win_ref, wpool_ref, pscale_ref,
                  gpool_ref, gattn_ref, wout_ref, post_ref, o_ref,
                  ubuf, kbuf, vbuf, qbuf, attn_buf):
    T = MIX_TOKENS
    s_idx = pl.program_id(1)
    x = x_ref[0]
    h = _rms(x, pre_ref[...]).astype(_BF16)
    z = jnp.dot(h, win_ref[...], preferred_element_type=_F32)

    @pl.when(s_idx == 0)
    def _():
        ubuf[0:POOL_HALO, :] = jnp.zeros((POOL_HALO, POOL_WIDTH), _F32)
        kbuf[:, 0:BLOCK, :] = jnp.zeros((4, BLOCK, LANES), _BF16)
        vbuf[:, 0:BLOCK, :] = jnp.zeros((4, BLOCK, LANES), _BF16)

    @pl.when(s_idx > 0)
    def _():
        ubuf[0:POOL_HALO, :] = ubuf[T:T + POOL_HALO, :]
        kbuf[:, 0:BLOCK, :] = kbuf[:, T:T + BLOCK, :]
        vbuf[:, 0:BLOCK, :] = vbuf[:, T:T + BLOCK, :]

    u = z[:, :POOL_WIDTH]
    ubuf[POOL_HALO:POOL_HALO + T, :] = u
    row = lax.broadcasted_iota(jnp.int32, (T, 1), 0)
    seen = s_idx * T + row + 1
    pooled = []
    for g, w in enumerate(POOL_WINDOWS):
        lo = g * POOL_GROUP
        win_sum = ubuf[:, lo:lo + POOL_GROUP]
        span = 1
        while span < w:
            win_sum = win_sum + pltpu.roll(win_sum, span, 0)
            span *= 2
        cnt = jnp.minimum(seen, w).astype(_F32)
        d = win_sum[POOL_HALO:, :] / cnt - u[:, lo:lo + POOL_GROUP]
        y = jnp.dot(d.astype(_BF16), wpool_ref[g], preferred_element_type=_F32)
        pooled.append(y)
    pool_out = jnp.concatenate(pooled, axis=1) * pscale_ref[...]

    lane = lax.broadcasted_iota(jnp.int32, (1, LANES), 1)
    dim = lane % HEAD_DIM
    lo_lanes = dim < ROT_DIM // 2
    ang = pos_ref[0].astype(_F32) * invf_ref[...]
    cos_t = jnp.cos(ang)
    sin_t = jnp.sin(ang) * jnp.where(lo_lanes, -1.0, 1.0)

    scale = HEAD_DIM ** -0.5
    for p in range(ATTN_WIDTH // LANES):
        qp = z[:, POOL_WIDTH + p * LANES:POOL_WIDTH + (p + 1) * LANES]
        qbuf[:, p * LANES:(p + 1) * LANES] = (_rope(qp, cos_t, sin_t, lo_lanes) * scale).astype(_BF16)

    first_half = lane < HEAD_DIM
    k = _rope(z[:, MIX_WIDTH:MIX_WIDTH + KV_WIDTH], cos_t, sin_t, lo_lanes)
    v = z[:, MIX_WIDTH + KV_WIDTH:]
    for buf, t in ((kbuf, k), (vbuf, v)):
        t_sw = pltpu.roll(t, HEAD_DIM, 1)
        variants = (jnp.where(first_half, t, 0.0), jnp.where(first_half, 0.0, t_sw),
                    jnp.where(first_half, t_sw, 0.0), jnp.where(first_half, 0.0, t))
        for idx, var in enumerate(variants):
            buf[idx, BLOCK:BLOCK + T, :] = var.astype(_BF16)

    qi = lax.broadcasted_iota(jnp.int32, (2 * BLOCK, 2 * BLOCK), 0) % BLOCK
    kj = lax.broadcasted_iota(jnp.int32, (2 * BLOCK, 2 * BLOCK), 1)
    band = (kj > qi) & (kj <= qi + BLOCK)
    band_first = band & ((kj >= BLOCK) | (s_idx > 0))
    upper_rows = lax.broadcasted_iota(jnp.int32, (2 * BLOCK, 1), 0) < BLOCK

    for i in range(T // BLOCK):
        r0 = i * BLOCK
        mask = band_first if i == 0 else band
        for kv in range(N_KV_HEADS):
            q2 = jnp.concatenate(
                [qbuf[r0:r0 + BLOCK, (2 * kv) * LANES:(2 * kv + 1) * LANES],
                 qbuf[r0:r0 + BLOCK, (2 * kv + 1) * LANES:(2 * kv + 2) * LANES]], axis=0)
            out2 = None
            for e in range(2):
                head_a = 4 * kv + e
                sink = jnp.where(upper_rows, sinks_ref[head_a], sinks_ref[head_a + 2])
                kwin = kbuf[2 * kv + e, r0:r0 + 2 * BLOCK, :]
                s = lax.dot_general(q2, kwin, (((1,), (1,)), ((), ())), preferred_element_type=_F32)
                s = jnp.where(mask, s, NEG_INF)
                m = jnp.maximum(jnp.max(s, axis=1, keepdims=True), sink)
                pexp = jnp.exp(s - m)
                denom = jnp.sum(pexp, axis=1, keepdims=True) + jnp.exp(sink - m)
                vwin = vbuf[2 * kv + e, r0:r0 + 2 * BLOCK, :]
                pv = jnp.dot(pexp.astype(_BF16), vwin, preferred_element_type=_F32) / denom
                out2 = pv if out2 is None else out2 + pv
            attn_buf[r0:r0 + BLOCK, (2 * kv) * LANES:(2 * kv + 1) * LANES] = out2[:BLOCK]
            attn_buf[r0:r0 + BLOCK, (2 * kv + 1) * LANES:(2 * kv + 2) * LANES] = out2[BLOCK:]

    y_pool = _rms(pool_out, gpool_ref[...]).astype(_BF16)
    y_attn = _rms(attn_buf[...], gattn_ref[...]).astype(_BF16)
    hout = (jnp.dot(y_pool, wout_ref[0:POOL_WIDTH, :], preferred_element_type=_F32)
            + jnp.dot(y_attn, wout_ref[POOL_WIDTH:, :], preferred_element_type=_F32))
    o_ref[0] = x + _rms(hout, post_ref[...])


def _mixer(x, positions, mix_pre, w_in, w_pool, pool_scale, sinks, g_pool, g_attn, w_out, mix_post):
    B, S, _ = x.shape
    T = MIX_TOKENS
    inv_freq = ROPE_THETA ** (-jnp.arange(0, ROT_DIM, 2, dtype=_F32) / ROT_DIM)
    dim = jnp.arange(LANES) % HEAD_DIM
    invf = jnp.where(dim < ROT_DIM, inv_freq[dim % (ROT_DIM // 2)], 0.0).reshape(1, LANES).astype(_F32)
    const = lambda nd: (lambda b, s, sinks_ref: (0,) * nd)
    row = lambda n: pl.BlockSpec((1, n), const(2))
    grid_spec = pltpu.PrefetchScalarGridSpec(
        num_scalar_prefetch=1,
        grid=(B, S // T),
        in_specs=[
            pl.BlockSpec((1, T, D_MODEL), lambda b, s, sinks_ref: (b, s, 0)),
            pl.BlockSpec((1, T, 1), lambda b, s, sinks_ref: (b, s, 0)),
            row(LANES),
            row(D_MODEL),
            pl.BlockSpec((D_MODEL, IN_WIDTH), const(2)),
            pl.BlockSpec((len(POOL_WINDOWS), POOL_GROUP, POOL_GROUP), const(3)),
            row(POOL_WIDTH),
            row(POOL_WIDTH),
            row(ATTN_WIDTH),
            pl.BlockSpec((MIX_WIDTH, D_MODEL), const(2)),
            row(D_MODEL),
        ],
        out_specs=pl.BlockSpec((1, T, D_MODEL), lambda b, s, sinks_ref: (b, s, 0)),
        scratch_shapes=[
            pltpu.VMEM((T + POOL_HALO, POOL_WIDTH), _F32),
            pltpu.VMEM((4, T + BLOCK, LANES), _BF16),
            pltpu.VMEM((4, T + BLOCK, LANES), _BF16),
            pltpu.VMEM((T, ATTN_WIDTH), _BF16),
            pltpu.VMEM((T, ATTN_WIDTH), _F32),
        ],
    )
    return pl.pallas_call(
        _mixer_kernel,
        grid_spec=grid_spec,
        out_shape=jax.ShapeDtypeStruct(x.shape, _F32),
        compiler_params=pltpu.CompilerParams(
            dimension_semantics=("arbitrary", "arbitrary"), vmem_limit_bytes=MIX_VMEM_BYTES),
    )(sinks.astype(_F32), x, positions.reshape(B, S, 1), invf, mix_pre.reshape(1, D_MODEL),
      w_in.astype(_BF16), w_pool.astype(_BF16), pool_scale.reshape(1, POOL_WIDTH),
      g_pool.reshape(1, POOL_WIDTH), g_attn.reshape(1, ATTN_WIDTH), w_out.astype(_BF16),
      mix_post.reshape(1, D_MODEL))


def kernel(x, positions, ffn1_pre, ffn1_w_gu, ffn1_w_down, ffn1_post, mix_pre, w_in, w_pool, pool_scale, sinks, g_pool, g_attn, w_out, mix_post, ffn2_pre, ffn2_w_gu, ffn2_w_down, ffn2_post):
    B, S, D = x.shape
    for l in range(ffn1_pre.shape[0]):
        x = _ffn(x.reshape(B * S, D), ffn1_pre[l], ffn1_w_gu[l], ffn1_w_down[l], ffn1_post[l]).reshape(B, S, D)
        x = _mixer(x, positions, mix_pre[l], w_in[l], w_pool[l], pool_scale[l], sinks[l], g_pool[l],
                   g_attn[l], w_out[l], mix_post[l])
        x = _ffn(x.reshape(B * S, D), ffn2_pre[l], ffn2_w_gu[l], ffn2_w_down[l], ffn2_post[l]).reshape(B, S, D)
    return x
```

```python
import functools

import jax
import jax.numpy as jnp
from jax import lax
from jax.experimental import pallas as pl
from jax.experimental.pallas import tpu as pltpu

D_MODEL = 1024
D_FF = 2816
POOL_WINDOWS = (2, 4, 8, 16)
POOL_WIDTH = 512
POOL_GROUP = 128
HEAD_DIM = 64
N_HEADS = 8
N_KV_HEADS = 2
ATTN_WIDTH = 512
KV_WIDTH = 128
BLOCK = 128
ROPE_THETA = 500000.0
ROT_DIM = 16
MIX_WIDTH = 1024
IN_WIDTH = 1280
EPS = 1e-6
NEG_INF = -1e30

LANES = 128
POOL_HALO = 16

FFN_TOKENS = 512
FFN_CHUNK = 256
MIX_TOKENS = 512
FFN_VMEM_BYTES = 48 * 1024 * 1024
MIX_VMEM_BYTES = 48 * 1024 * 1024

_F32 = jnp.float32
_BF16 = jnp.bfloat16


def _rms(x, g):
    return x * lax.rsqrt(jnp.mean(x * x, axis=-1, keepdims=True) + EPS) * g


def _ffn_kernel(x_ref, pre_ref, wgu_ref, wd_ref, post_ref, o_ref, xn_ref, act_ref):
    x = x_ref[...]
    xn_ref[...] = _rms(x, pre_ref[...]).astype(_BF16)
    for c in range(D_FF // FFN_CHUNK):
        lo = c * FFN_CHUNK
        xn = xn_ref[...]
        g = jnp.dot(xn, wgu_ref[:, lo:lo + FFN_CHUNK], preferred_element_type=_F32)
        u = jnp.dot(xn, wgu_ref[:, D_FF + lo:D_FF + lo + FFN_CHUNK], preferred_element_type=_F32)
        act_ref[:, lo:lo + FFN_CHUNK] = (g * jax.nn.sigmoid(g) * u).astype(_BF16)
    h = jnp.dot(act_ref[...], wd_ref[...], preferred_element_type=_F32)
    o_ref[...] = x + 0.5 * _rms(h, post_ref[...])


def _ffn(x2d, pre, w_gu, w_down, post):
    n_tok = x2d.shape[0]
    const = lambda i: (0, 0)
    resident = dict(pipeline_mode=pl.Buffered(1))
    return pl.pallas_call(
        _ffn_kernel,
        name="ffn",
        grid=(n_tok // FFN_TOKENS,),
        in_specs=[
            pl.BlockSpec((FFN_TOKENS, D_MODEL), lambda i: (i, 0)),
            pl.BlockSpec((1, D_MODEL), const, **resident),
            pl.BlockSpec((D_MODEL, 2 * D_FF), const, **resident),
            pl.BlockSpec((D_FF, D_MODEL), const, **resident),
            pl.BlockSpec((1, D_MODEL), const, **resident),
        ],
        out_specs=pl.BlockSpec((FFN_TOKENS, D_MODEL), lambda i: (i, 0)),
        out_shape=jax.ShapeDtypeStruct(x2d.shape, _F32),
        scratch_shapes=[
            pltpu.VMEM((FFN_TOKENS, D_MODEL), _BF16),
            pltpu.VMEM((FFN_TOKENS, D_FF), _BF16),
        ],
        compiler_params=pltpu.CompilerParams(
            dimension_semantics=("arbitrary",), vmem_limit_bytes=FFN_VMEM_BYTES),
    )(x2d, pre.reshape(1, D_MODEL), w_gu.astype(_BF16), w_down.astype(_BF16), post.reshape(1, D_MODEL))


def _rope(t, cos_t, sin_t, lo_lanes):
    partner = jnp.where(lo_lanes, pltpu.roll(t, LANES - ROT_DIM // 2, 1), pltpu.roll(t, ROT_DIM // 2, 1))
    return t * cos_t + partner * sin_t


def _mixer_kernel(sinks_ref, x_ref, pos_ref, invf_ref, pre_ref, win_ref, wpool_ref, pscale_ref,
                  gpool_ref, gattn_ref, wout_ref, post_ref, o_ref,
                  ubuf, kbuf, vbuf, qbuf, attn_buf):
    T = MIX_TOKENS
    s_idx = pl.program_id(1)
    x = x_ref[0]
    h = _rms(x, pre_ref[...]).astype(_BF16)
    z = jnp.dot(h, win_ref[...], preferred_element_type=_F32)

    @pl.when(s_idx == 0)
    def _():
        ubuf[0:POOL_HALO, :] = jnp.zeros((POOL_HALO, POOL_WIDTH), _F32)
        kbuf[:, 0:BLOCK, :] = jnp.zeros((4, BLOCK, LANES), _BF16)
        vbuf[:, 0:BLOCK, :] = jnp.zeros((4, BLOCK, LANES), _BF16)

    @pl.when(s_idx > 0)
    def _():
        ubuf[0:POOL_HALO, :] = ubuf[T:T + POOL_HALO, :]
        kbuf[:, 0:BLOCK, :] = kbuf[:, T:T + BLOCK, :]
        vbuf[:, 0:BLOCK, :] = vbuf[:, T:T + BLOCK, :]

    u = z[:, :POOL_WIDTH]
    ubuf[POOL_HALO:POOL_HALO + T, :] = u
    row = lax.broadcasted_iota(jnp.int32, (T, 1), 0)
    seen = s_idx * T + row + 1
    pooled = []
    for g, w in enumerate(POOL_WINDOWS):
        lo = g * POOL_GROUP
        win_sum = ubuf[:, lo:lo + POOL_GROUP]
        span = 1
        while span < w:
            win_sum = win_sum + pltpu.roll(win_sum, span, 0)
            span *= 2
        cnt = jnp.minimum(seen, w).astype(_F32)
        d = win_sum[POOL_HALO:, :] / cnt - u[:, lo:lo + POOL_GROUP]
        y = jnp.dot(d.astype(_BF16), wpool_ref[g], preferred_element_type=_F32)
        pooled.append(y)
    pool_out = jnp.concatenate(pooled, axis=1) * pscale_ref[...]

    lane = lax.broadcasted_iota(jnp.int32, (1, LANES), 1)
    dim = lane % HEAD_DIM
    lo_lanes = dim < ROT_DIM // 2
    ang = pos_ref[0].astype(_F32) * invf_ref[...]
    cos_t = jnp.cos(ang)
    sin_t = jnp.sin(ang) * jnp.where(lo_lanes, -1.0, 1.0)

    scale = HEAD_DIM ** -0.5
    for p in range(ATTN_WIDTH // LANES):
        qp = z[:, POOL_WIDTH + p * LANES:POOL_WIDTH + (p + 1) * LANES]
        qbuf[:, p * LANES:(p + 1) * LANES] = (_rope(qp, cos_t, sin_t, lo_lanes) * scale).astype(_BF16)

    first_half = lane < HEAD_DIM
    k = _rope(z[:, MIX_WIDTH:MIX_WIDTH + KV_WIDTH], cos_t, sin_t, lo_lanes)
    v = z[:, MIX_WIDTH + KV_WIDTH:]
    for buf, t in ((kbuf, k), (vbuf, v)):
        t_sw = pltpu.roll(t, HEAD_DIM, 1)
        variants = (jnp.where(first_half, t, 0.0), jnp.where(first_half, 0.0, t_sw),
                    jnp.where(first_half, t_sw, 0.0), jnp.where(first_half, 0.0, t))
        for idx, var in enumerate(variants):
            buf[idx, BLOCK:BLOCK + T, :] = var.astype(_BF16)

    qi = lax.broadcasted_iota(jnp.int32, (2 * BLOCK, 2 * BLOCK), 0) % BLOCK
    kj = lax.broadcasted_iota(jnp.int32, (2 * BLOCK, 2 * BLOCK), 1)
    band = (kj > qi) & (kj <= qi + BLOCK)
    band_first = band & ((kj >= BLOCK) | (s_idx > 0))
    upper_rows = lax.broadcasted_iota(jnp.int32, (2 * BLOCK, 1), 0) < BLOCK

    for i in range(T // BLOCK):
        r0 = i * BLOCK
        mask = band_first if i == 0 else band
        for kv in range(N_KV_HEADS):
            q2 = jnp.concatenate(
                [qbuf[r0:r0 + BLOCK, (2 * kv) * LANES:(2 * kv + 1) * LANES],
                 qbuf[r0:r0 + BLOCK, (2 * kv + 1) * LANES:(2 * kv + 2) * LANES]], axis=0)
            out2 = None
            for e in range(2):
                head_a = 4 * kv + e
                sink = jnp.where(upper_rows, sinks_ref[head_a], sinks_ref[head_a + 2])
                kwin = kbuf[2 * kv + e, r0:r0 + 2 * BLOCK, :]
                s = lax.dot_general(q2, kwin, (((1,), (1,)), ((), ())), preferred_element_type=_F32)
                s = jnp.where(mask, s, NEG_INF)
                m = jnp.maximum(jnp.max(s, axis=1, keepdims=True), sink)
                pexp = jnp.exp(s - m)
                denom = jnp.sum(pexp, axis=1, keepdims=True) + jnp.exp(sink - m)
                vwin = vbuf[2 * kv + e, r0:r0 + 2 * BLOCK, :]
                pv = jnp.dot(pexp.astype(_BF16), vwin, preferred_element_type=_F32) / denom
                out2 = pv if out2 is None else out2 + pv
            attn_buf[r0:r0 + BLOCK, (2 * kv) * LANES:(2 * kv + 1) * LANES] = out2[:BLOCK]
            attn_buf[r0:r0 + BLOCK, (2 * kv + 1) * LANES:(2 * kv + 2) * LANES] = out2[BLOCK:]

    y_pool = _rms(pool_out, gpool_ref[...]).astype(_BF16)
    y_attn = _rms(attn_buf[...], gattn_ref[...]).astype(_BF16)
    hout = (jnp.dot(y_pool, wout_ref[0:POOL_WIDTH, :], preferred_element_type=_F32)
            + jnp.dot(y_attn, wout_ref[POOL_WIDTH:, :], preferred_element_type=_F32))
    o_ref[0] = x + _rms(hout, post_ref[...])


def _mixer(x, positions, mix_pre, w_in, w_pool, pool_scale, sinks, g_pool, g_attn, w_out, mix_post):
    B, S, _ = x.shape
    T = MIX_TOKENS
    inv_freq = ROPE_THETA ** (-jnp.arange(0, ROT_DIM, 2, dtype=_F32) / ROT_DIM)
    dim = jnp.arange(LANES) % HEAD_DIM
    invf = jnp.where(dim < ROT_DIM, inv_freq[dim % (ROT_DIM // 2)], 0.0).reshape(1, LANES).astype(_F32)
    const = lambda nd: (lambda b, s, sinks_ref: (0,) * nd)
    row = lambda n: pl.BlockSpec((1, n), const(2))
    grid_spec = pltpu.PrefetchScalarGridSpec(
        num_scalar_prefetch=1,
        grid=(B, S // T),
        in_specs=[
            pl.BlockSpec((1, T, D_MODEL), lambda b, s, sinks_ref: (b, s, 0)),
            pl.BlockSpec((1, T, 1), lambda b, s, sinks_ref: (b, s, 0)),
            row(LANES),
            row(D_MODEL),
            pl.BlockSpec((D_MODEL, IN_WIDTH), const(2)),
            pl.BlockSpec((len(POOL_WINDOWS), POOL_GROUP, POOL_GROUP), const(3)),
            row(POOL_WIDTH),
            row(POOL_WIDTH),
            row(ATTN_WIDTH),
            pl.BlockSpec((MIX_WIDTH, D_MODEL), const(2)),
            row(D_MODEL),
        ],
        out_specs=pl.BlockSpec((1, T, D_MODEL), lambda b, s, sinks_ref: (b, s, 0)),
        scratch_shapes=[
            pltpu.VMEM((T + POOL_HALO, POOL_WIDTH), _F32),
            pltpu.VMEM((4, T + BLOCK, LANES), _BF16),
            pltpu.VMEM((4, T + BLOCK, LANES), _BF16),
            pltpu.VMEM((T, ATTN_WIDTH), _BF16),
            pltpu.VMEM((T, ATTN_WIDTH), _F32),
        ],
    )
    return pl.pallas_call(
        _mixer_kernel,
        name="mixer",
        grid_spec=grid_spec,
        out_shape=jax.ShapeDtypeStruct(x.shape, _F32),
        compiler_params=pltpu.CompilerParams(
            dimension_semantics=("arbitrary", "arbitrary"), vmem_limit_bytes=MIX_VMEM_BYTES),
    )(sinks.astype(_F32), x, positions.reshape(B, S, 1), invf, mix_pre.reshape(1, D_MODEL),
      w_in.astype(_BF16), w_pool.astype(_BF16), pool_scale.reshape(1, POOL_WIDTH),
      g_pool.reshape(1, POOL_WIDTH), g_attn.reshape(1, ATTN_WIDTH), w_out.astype(_BF16),
      mix_post.reshape(1, D_MODEL))


def kernel(x, positions, ffn1_pre, ffn1_w_gu, ffn1_w_down, ffn1_post, mix_pre, w_in, w_pool, pool_scale, sinks, g_pool, g_attn, w_out, mix_post, ffn2_pre, ffn2_w_gu, ffn2_w_down, ffn2_post):
    B, S, D = x.shape
    for l in range(ffn1_pre.shape[0]):
        x = _ffn(x.reshape(B * S, D), ffn1_pre[l], ffn1_w_gu[l], ffn1_w_down[l], ffn1_post[l]).reshape(B, S, D)
        x = _mixer(x, positions, mix_pre[l], w_in[l], w_pool[l], pool_scale[l], sinks[l], g_pool[l],
                   g_attn[l], w_out[l], mix_post[l])
        x = _ffn(x.reshape(B * S, D), ffn2_pre[l], ffn2_w_gu[l], ffn2_w_down[l], ffn2_post[l]).reshape(B, S, D)
    return x
```

```python
import functools

import jax
import jax.numpy as jnp
from jax import lax
from jax.experimental import pallas as pl
from jax.experimental.pallas import tpu as pltpu

D_MODEL = 1024
D_FF = 2816
POOL_WINDOWS = (2, 4, 8, 16)
POOL_WIDTH = 512
POOL_GROUP = 128
HEAD_DIM = 64
N_HEADS = 8
N_KV_HEADS = 2
ATTN_WIDTH = 512
KV_WIDTH = 128
BLOCK = 128
ROPE_THETA = 500000.0
ROT_DIM = 16
MIX_WIDTH = 1024
IN_WIDTH = 1280
EPS = 1e-6
NEG_INF = -1e30

LANES = 128
POOL_HALO = 16

FFN_TOKENS = 512
FFN_CHUNK = 256
MIX_TOKENS = 512
FFN_VMEM_BYTES = 48 * 1024 * 1024
MIX_VMEM_BYTES = 48 * 1024 * 1024

_F32 = jnp.float32
_BF16 = jnp.bfloat16


def _rms(x, g):
    return x * lax.rsqrt(jnp.mean(x * x, axis=-1, keepdims=True) + EPS) * g


def _ffn_kernel(x_ref, pre_ref, wgu_ref, wd_ref, post_ref, o_ref, xn_ref, act_ref):
    x = x_ref[...]
    xn_ref[...] = _rms(x, pre_ref[...]).astype(_BF16)
    for c in range(D_FF // FFN_CHUNK):
        lo = c * FFN_CHUNK
        xn = xn_ref[...]
        g = jnp.dot(xn, wgu_ref[:, lo:lo + FFN_CHUNK], preferred_element_type=_F32)
        u = jnp.dot(xn, wgu_ref[:, D_FF + lo:D_FF + lo + FFN_CHUNK], preferred_element_type=_F32)
        act_ref[:, lo:lo + FFN_CHUNK] = (g * jax.nn.sigmoid(g) * u).astype(_BF16)
    h = jnp.dot(act_ref[...], wd_ref[...], preferred_element_type=_F32)
    o_ref[...] = x + 0.5 * _rms(h, post_ref[...])


def _ffn(x2d, pre, w_gu, w_down, post):
    n_tok = x2d.shape[0]
    const = lambda i: (0, 0)
    resident = dict(pipeline_mode=pl.Buffered(1))
    return pl.pallas_call(
        _ffn_kernel,
        name="ffn",
        grid=(n_tok // FFN_TOKENS,),
        in_specs=[
            pl.BlockSpec((FFN_TOKENS, D_MODEL), lambda i: (i, 0)),
            pl.BlockSpec((1, D_MODEL), const, **resident),
            pl.BlockSpec((D_MODEL, 2 * D_FF), const, **resident),
            pl.BlockSpec((D_FF, D_MODEL), const, **resident),
            pl.BlockSpec((1, D_MODEL), const, **resident),
        ],
        out_specs=pl.BlockSpec((FFN_TOKENS, D_MODEL), lambda i: (i, 0)),
        out_shape=jax.ShapeDtypeStruct(x2d.shape, _F32),
        scratch_shapes=[
            pltpu.VMEM((FFN_TOKENS, D_MODEL), _BF16),
            pltpu.VMEM((FFN_TOKENS, D_FF), _BF16),
        ],
        compiler_params=pltpu.CompilerParams(
            dimension_semantics=("arbitrary",), vmem_limit_bytes=FFN_VMEM_BYTES),
    )(x2d, pre.reshape(1, D_MODEL), w_gu.astype(_BF16), w_down.astype(_BF16), post.reshape(1, D_MODEL))


def _rope(t, cos_t, sin_t, lo_lanes):
    partner = jnp.where(lo_lanes, pltpu.roll(t, LANES - ROT_DIM // 2, 1), pltpu.roll(t, ROT_DIM // 2, 1))
    return t * cos_t + partner * sin_t


def _mixer_kernel(sinks_ref, x_ref, posl_ref, invf_ref, place_ref, pre_ref, win_ref, wpool_ref, pscale_ref,
                  gpool_ref, gattn_ref, wout_ref, post_ref, o_ref,
                  ubuf, kbuf, vbuf, qbuf, attn_buf, band_ref, sbuf, pbuf, ibuf):
    T = MIX_TOKENS
    s_idx = pl.program_id(1)
    x = x_ref[0]
    h = _rms(x, pre_ref[...]).astype(_BF16)
    z = jnp.dot(h, win_ref[...], preferred_element_type=_F32)

    @pl.when(s_idx == 0)
    def _():
        ubuf[0:POOL_HALO, :] = jnp.zeros((POOL_HALO, POOL_WIDTH), _F32)
        kbuf[:, 0:BLOCK, :] = jnp.zeros((4, BLOCK, LANES), _BF16)
        vbuf[:, 0:BLOCK, :] = jnp.zeros((4, BLOCK, LANES), _BF16)

    @pl.when(s_idx > 0)
    def _():
        ubuf[0:POOL_HALO, :] = ubuf[T:T + POOL_HALO, :]
        kbuf[:, 0:BLOCK, :] = kbuf[:, T:T + BLOCK, :]
        vbuf[:, 0:BLOCK, :] = vbuf[:, T:T + BLOCK, :]

    u = z[:, :POOL_WIDTH]
    ubuf[POOL_HALO:POOL_HALO + T, :] = u
    row = lax.broadcasted_iota(jnp.int32, (T, 1), 0)
    seen = s_idx * T + row + 1
    pooled = []
    for g, w in enumerate(POOL_WINDOWS):
        lo = g * POOL_GROUP
        win_sum = ubuf[:, lo:lo + POOL_GROUP]
        span = 1
        while span < w:
            win_sum = win_sum + pltpu.roll(win_sum, span, 0)
            span *= 2
        cnt = jnp.minimum(seen, w).astype(_F32)
        d = win_sum[POOL_HALO:, :] / cnt - u[:, lo:lo + POOL_GROUP]
        y = jnp.dot(d.astype(_BF16), wpool_ref[g], preferred_element_type=_F32)
        pooled.append(y)
    pool_out = jnp.concatenate(pooled, axis=1) * pscale_ref[...]

    lane = lax.broadcasted_iota(jnp.int32, (1, LANES), 1)
    dim = lane % HEAD_DIM
    lo_lanes = dim < ROT_DIM // 2
    ang = posl_ref[0].astype(_F32) * invf_ref[...]
    cs = jnp.concatenate([jnp.cos(ang), jnp.sin(ang)], axis=0)
    cs_hi = cs.astype(_BF16)
    rem = cs - cs_hi.astype(_F32)
    cs_mid = rem.astype(_BF16)
    cs_lo = (rem - cs_mid.astype(_F32)).astype(_BF16)
    cs3 = jnp.concatenate([cs_hi, cs_mid, cs_lo], axis=0)
    tables = lax.dot_general(cs3, place_ref[...], (((0,), (0,)), ((), ())), preferred_element_type=_F32)
    cos_t = jnp.where(dim < ROT_DIM, tables[:, :LANES], 1.0)
    sin_t = tables[:, LANES:]

    scale = HEAD_DIM ** -0.5
    for p in range(ATTN_WIDTH // LANES):
        qp = z[:, POOL_WIDTH + p * LANES:POOL_WIDTH + (p + 1) * LANES]
        qbuf[p] = (_rope(qp, cos_t, sin_t, lo_lanes) * scale).astype(_BF16)

    first_half = lane < HEAD_DIM
    k = _rope(z[:, MIX_WIDTH:MIX_WIDTH + KV_WIDTH], cos_t, sin_t, lo_lanes)
    v = z[:, MIX_WIDTH + KV_WIDTH:]
    for buf, t in ((kbuf, k), (vbuf, v)):
        t_sw = pltpu.roll(t, HEAD_DIM, 1)
        variants = (jnp.where(first_half, t, 0.0), jnp.where(first_half, 0.0, t_sw),
                    jnp.where(first_half, t_sw, 0.0), jnp.where(first_half, 0.0, t))
        for idx, var in enumerate(variants):
            buf[idx, BLOCK:BLOCK + T, :] = var.astype(_BF16)

    @pl.when((pl.program_id(0) == 0) & (s_idx == 0))
    def _():
        qi = lax.broadcasted_iota(jnp.int32, (2 * BLOCK, 2 * BLOCK), 0) % BLOCK
        kj = lax.broadcasted_iota(jnp.int32, (2 * BLOCK, 2 * BLOCK), 1)
        band = (kj > qi) & (kj <= qi + BLOCK)
        band_ref[0] = band.astype(_F32)
        band_ref[1] = (band & (kj >= BLOCK)).astype(_F32)

    upper_rows = lax.broadcasted_iota(jnp.int32, (2 * BLOCK, 1), 0) < BLOCK
    nt = (((1,), (1,)), ((), ()))
    n_units = N_KV_HEADS * T // BLOCK

    def rows(u, n):
        start = (u // N_KV_HEADS) * BLOCK
        return pl.ds(start if isinstance(u, int) else pl.multiple_of(start, BLOCK), n)

    def scores(u):
        kv, slot = u % N_KV_HEADS, u % 2
        q2 = jnp.concatenate([qbuf[2 * kv, rows(u, BLOCK), :], qbuf[2 * kv + 1, rows(u, BLOCK), :]], axis=0)
        first = (s_idx == 0) & (u < N_KV_HEADS)
        valid = band_ref[jnp.where(first, 1, 0)] > 0.5
        for e in range(2):
            s = lax.dot_general(q2, kbuf[2 * kv + e, rows(u, 2 * BLOCK), :], nt, preferred_element_type=_F32)
            sbuf[slot, e] = jnp.where(valid, s, NEG_INF)

    def softmax(u):
        kv, slot = u % N_KV_HEADS, u % 2
        for e in range(2):
            head_a = 4 * kv + e
            sink = jnp.where(upper_rows, sinks_ref[head_a], sinks_ref[head_a + 2])
            s = sbuf[slot, e]
            m = jnp.maximum(jnp.max(s, axis=1, keepdims=True), sink)
            pexp = jnp.exp(s - m)
            pbuf[slot, e] = pexp.astype(_BF16)
            denom = jnp.sum(pexp, axis=1, keepdims=True) + jnp.exp(sink - m)
            ibuf[slot, e] = jnp.broadcast_to(1.0 / denom, (2 * BLOCK, LANES))

    def weighted(u):
        kv, slot = u % N_KV_HEADS, u % 2
        out2 = None
        for e in range(2):
            pv = jnp.dot(pbuf[slot, e], vbuf[2 * kv + e, rows(u, 2 * BLOCK), :], preferred_element_type=_F32)
            pv = pv * ibuf[slot, e]
            out2 = pv if out2 is None else out2 + pv
        attn_buf[2 * kv, rows(u, BLOCK), :] = out2[:BLOCK]
        attn_buf[2 * kv + 1, rows(u, BLOCK), :] = out2[BLOCK:]

    scores(0)
    scores(1)
    softmax(0)

    def steady(u, carry):
        weighted(u - 2)
        softmax(u - 1)
        scores(u)
        return carry

    lax.fori_loop(2, n_units, steady, 0)
    softmax(n_units - 1)
    weighted(n_units - 2)
    weighted(n_units - 1)

    y_pool = _rms(pool_out, gpool_ref[...]).astype(_BF16)
    attn = jnp.concatenate([attn_buf[p] for p in range(ATTN_WIDTH // LANES)], axis=1)
    y_attn = _rms(attn, gattn_ref[...]).astype(_BF16)
    hout = (jnp.dot(y_pool, wout_ref[0:POOL_WIDTH, :], preferred_element_type=_F32)
            + jnp.dot(y_attn, wout_ref[POOL_WIDTH:, :], preferred_element_type=_F32))
    o_ref[0] = x + _rms(hout, post_ref[...])


def _mixer(x, positions, mix_pre, w_in, w_pool, pool_scale, sinks, g_pool, g_attn, w_out, mix_post):
    B, S, _ = x.shape
    T = MIX_TOKENS
    half = ROT_DIM // 2
    inv_freq = ROPE_THETA ** (-jnp.arange(0, ROT_DIM, 2, dtype=_F32) / ROT_DIM)
    dim = jnp.arange(LANES) % HEAD_DIM
    freq = jnp.arange(half)[:, None]
    cos_place = ((dim[None, :] < ROT_DIM) & (dim[None, :] % half == freq)).astype(_F32)
    sin_place = cos_place * jnp.where(dim < half, -1.0, 1.0)[None, :]
    zeros = jnp.zeros_like(cos_place)
    place = jnp.concatenate([jnp.concatenate([cos_place, zeros], axis=1),
                             jnp.concatenate([zeros, sin_place], axis=1)], axis=0)
    place = jnp.tile(place, (3, 1)).astype(_BF16)
    const = lambda nd: (lambda b, s, sinks_ref: (0,) * nd)
    row = lambda n: pl.BlockSpec((1, n), const(2))
    grid_spec = pltpu.PrefetchScalarGridSpec(
        num_scalar_prefetch=1,
        grid=(B, S // T),
        in_specs=[
            pl.BlockSpec((1, T, D_MODEL), lambda b, s, sinks_ref: (b, s, 0)),
            pl.BlockSpec((1, 1, T), lambda b, s, sinks_ref: (b, 0, s)),
            pl.BlockSpec((ROT_DIM // 2, 1), const(2)),
            pl.BlockSpec((3 * ROT_DIM, 2 * LANES), const(2)),
            row(D_MODEL),
            pl.BlockSpec((D_MODEL, IN_WIDTH), const(2)),
            pl.BlockSpec((len(POOL_WINDOWS), POOL_GROUP, POOL_GROUP), const(3)),
            row(POOL_WIDTH),
            row(POOL_WIDTH),
            row(ATTN_WIDTH),
            pl.BlockSpec((MIX_WIDTH, D_MODEL), const(2)),
            row(D_MODEL),
        ],
        out_specs=pl.BlockSpec((1, T, D_MODEL), lambda b, s, sinks_ref: (b, s, 0)),
        scratch_shapes=[
            pltpu.VMEM((T + POOL_HALO, POOL_WIDTH), _F32),
            pltpu.VMEM((4, T + BLOCK, LANES), _BF16),
            pltpu.VMEM((4, T + BLOCK, LANES), _BF16),
            pltpu.VMEM((ATTN_WIDTH // LANES, T, LANES), _BF16),
            pltpu.VMEM((ATTN_WIDTH // LANES, T, LANES), _F32),
            pltpu.VMEM((2, 2 * BLOCK, 2 * BLOCK), _F32),
            pltpu.VMEM((2, 2, 2 * BLOCK, 2 * BLOCK), _F32),
            pltpu.VMEM((2, 2, 2 * BLOCK, 2 * BLOCK), _BF16),
            pltpu.VMEM((2, 2, 2 * BLOCK, LANES), _F32),
        ],
    )
    return pl.pallas_call(
        _mixer_kernel,
        name="mixer",
        grid_spec=grid_spec,
        out_shape=jax.ShapeDtypeStruct(x.shape, _F32),
        compiler_params=pltpu.CompilerParams(
            dimension_semantics=("arbitrary", "arbitrary"), vmem_limit_bytes=MIX_VMEM_BYTES),
    )(sinks.astype(_F32), x, positions.reshape(B, 1, S), inv_freq.reshape(half, 1), place, mix_pre.reshape(1, D_MODEL),
      w_in.astype(_BF16), w_pool.astype(_BF16), pool_scale.reshape(1, POOL_WIDTH),
      g_pool.reshape(1, POOL_WIDTH), g_attn.reshape(1, ATTN_WIDTH), w_out.astype(_BF16),
      mix_post.reshape(1, D_MODEL))


def kernel(x, positions, ffn1_pre, ffn1_w_gu, ffn1_w_down, ffn1_post, mix_pre, w_in, w_pool, pool_scale, sinks, g_pool, g_attn, w_out, mix_post, ffn2_pre, ffn2_w_gu, ffn2_w_down, ffn2_post):
    B, S, D = x.shape
    for l in range(ffn1_pre.shape[0]):
        x = _ffn(x.reshape(B * S, D), ffn1_pre[l], ffn1_w_gu[l], ffn1_w_down[l], ffn1_post[l]).reshape(B, S, D)
        x = _mixer(x, positions, mix_pre[l], w_in[l], w_pool[l], pool_scale[l], sinks[l], g_pool[l],
                   g_attn[l], w_out[l], mix_post[l])
        x = _ffn(x.reshape(B * S, D), ffn2_pre[l], ffn2_w_gu[l], ffn2_w_down[l], ffn2_post[l]).reshape(B, S, D)
    return x
```

```python
import functools

import jax
import jax.numpy as jnp
from jax import lax
from jax.experimental import pallas as pl
from jax.experimental.pallas import tpu as pltpu

D_MODEL = 1024
D_FF = 2816
POOL_WINDOWS = (2, 4, 8, 16)
POOL_WIDTH = 512
POOL_GROUP = 128
HEAD_DIM = 64
N_HEADS = 8
N_KV_HEADS = 2
ATTN_WIDTH = 512
KV_WIDTH = 128
BLOCK = 128
ROPE_THETA = 500000.0
ROT_DIM = 16
MIX_WIDTH = 1024
IN_WIDTH = 1280
EPS = 1e-6
NEG_INF = -1e30
LOG2E = 1.4426950408889634

LANES = 128
POOL_HALO = 16

FFN_TOKENS = 512
FFN_CHUNK = 256
MIX_TOKENS = 512
FFN_VMEM_BYTES = 48 * 1024 * 1024
MIX_VMEM_BYTES = 48 * 1024 * 1024

_F32 = jnp.float32
_BF16 = jnp.bfloat16


def _rms(x, g):
    return x * lax.rsqrt(jnp.mean(x * x, axis=-1, keepdims=True) + EPS) * g


def _ffn_kernel(x_ref, pre_ref, wgu_ref, wd_ref, post_ref, o_ref, xn_ref, act_ref):
    x = x_ref[...]
    xn_ref[...] = _rms(x, pre_ref[...]).astype(_BF16)
    for c in range(D_FF // FFN_CHUNK):
        lo = c * FFN_CHUNK
        xn = xn_ref[...]
        g = jnp.dot(xn, wgu_ref[:, lo:lo + FFN_CHUNK], preferred_element_type=_F32)
        u = jnp.dot(xn, wgu_ref[:, D_FF + lo:D_FF + lo + FFN_CHUNK], preferred_element_type=_F32)
        act_ref[:, lo:lo + FFN_CHUNK] = (g * jax.nn.sigmoid(g) * u).astype(_BF16)
    h = jnp.dot(act_ref[...], wd_ref[...], preferred_element_type=_F32)
    o_ref[...] = x + 0.5 * _rms(h, post_ref[...])


def _ffn(x2d, pre, w_gu, w_down, post):
    n_tok = x2d.shape[0]
    const = lambda i: (0, 0)
    resident = dict(pipeline_mode=pl.Buffered(1))
    return pl.pallas_call(
        _ffn_kernel,
        name="ffn",
        grid=(n_tok // FFN_TOKENS,),
        in_specs=[
            pl.BlockSpec((FFN_TOKENS, D_MODEL), lambda i: (i, 0)),
            pl.BlockSpec((1, D_MODEL), const, **resident),
            pl.BlockSpec((D_MODEL, 2 * D_FF), const, **resident),
            pl.BlockSpec((D_FF, D_MODEL), const, **resident),
            pl.BlockSpec((1, D_MODEL), const, **resident),
        ],
        out_specs=pl.BlockSpec((FFN_TOKENS, D_MODEL), lambda i: (i, 0)),
        out_shape=jax.ShapeDtypeStruct(x2d.shape, _F32),
        scratch_shapes=[
            pltpu.VMEM((FFN_TOKENS, D_MODEL), _BF16),
            pltpu.VMEM((FFN_TOKENS, D_FF), _BF16),
        ],
        compiler_params=pltpu.CompilerParams(
            dimension_semantics=("arbitrary",), vmem_limit_bytes=FFN_VMEM_BYTES),
    )(x2d, pre.reshape(1, D_MODEL), w_gu.astype(_BF16), w_down.astype(_BF16), post.reshape(1, D_MODEL))


def _rope(t, cos_t, sin_t, lo_lanes):
    partner = jnp.where(lo_lanes, pltpu.roll(t, LANES - ROT_DIM // 2, 1), pltpu.roll(t, ROT_DIM // 2, 1))
    return t * cos_t + partner * sin_t


def _mixer_kernel(sinks_ref, x_ref, posl_ref, invf_ref, place_ref, pre_ref, win_ref, wpool_ref, pscale_ref,
                  gpool_ref, gattn_ref, wout_ref, post_ref, o_ref,
                  ubuf, kbuf, vbuf, qbuf, attn_buf, band_ref, sbuf, pbuf, ibuf):
    T = MIX_TOKENS
    s_idx = pl.program_id(1)
    x = x_ref[0]
    h = _rms(x, pre_ref[...]).astype(_BF16)
    z = jnp.dot(h, win_ref[...], preferred_element_type=_F32)

    @pl.when(s_idx == 0)
    def _():
        ubuf[0:POOL_HALO, :] = jnp.zeros((POOL_HALO, POOL_WIDTH), _F32)
        kbuf[:, 0:BLOCK, :] = jnp.zeros((4, BLOCK, LANES), _BF16)
        vbuf[:, 0:BLOCK, :] = jnp.zeros((4, BLOCK, LANES), _BF16)

    @pl.when(s_idx > 0)
    def _():
        ubuf[0:POOL_HALO, :] = ubuf[T:T + POOL_HALO, :]
        kbuf[:, 0:BLOCK, :] = kbuf[:, T:T + BLOCK, :]
        vbuf[:, 0:BLOCK, :] = vbuf[:, T:T + BLOCK, :]

    u = z[:, :POOL_WIDTH]
    ubuf[POOL_HALO:POOL_HALO + T, :] = u
    row = lax.broadcasted_iota(jnp.int32, (T, 1), 0)
    seen = s_idx * T + row + 1
    pooled = []
    for g, w in enumerate(POOL_WINDOWS):
        lo = g * POOL_GROUP
        win_sum = ubuf[:, lo:lo + POOL_GROUP]
        span = 1
        while span < w:
            win_sum = win_sum + pltpu.roll(win_sum, span, 0)
            span *= 2
        cnt = jnp.minimum(seen, w).astype(_F32)
        d = win_sum[POOL_HALO:, :] / cnt - u[:, lo:lo + POOL_GROUP]
        y = jnp.dot(d.astype(_BF16), wpool_ref[g], preferred_element_type=_F32)
        pooled.append(y)
    pool_out = jnp.concatenate(pooled, axis=1) * pscale_ref[...]

    lane = lax.broadcasted_iota(jnp.int32, (1, LANES), 1)
    dim = lane % HEAD_DIM
    lo_lanes = dim < ROT_DIM // 2
    ang = posl_ref[0].astype(_F32) * invf_ref[...]
    cs = jnp.concatenate([jnp.cos(ang), jnp.sin(ang)], axis=0)
    cs_hi = cs.astype(_BF16)
    rem = cs - cs_hi.astype(_F32)
    cs_mid = rem.astype(_BF16)
    cs_lo = (rem - cs_mid.astype(_F32)).astype(_BF16)
    cs3 = jnp.concatenate([cs_hi, cs_mid, cs_lo], axis=0)
    tables = lax.dot_general(cs3, place_ref[...], (((0,), (0,)), ((), ())), preferred_element_type=_F32)
    cos_t = jnp.where(dim < ROT_DIM, tables[:, :LANES], 1.0)
    sin_t = tables[:, LANES:]

    scale = HEAD_DIM ** -0.5 * LOG2E
    for p in range(ATTN_WIDTH // LANES):
        qp = z[:, POOL_WIDTH + p * LANES:POOL_WIDTH + (p + 1) * LANES]
        qbuf[p] = (_rope(qp, cos_t, sin_t, lo_lanes) * scale).astype(_BF16)

    first_half = lane < HEAD_DIM
    k = _rope(z[:, MIX_WIDTH:MIX_WIDTH + KV_WIDTH], cos_t, sin_t, lo_lanes)
    v = z[:, MIX_WIDTH + KV_WIDTH:]
    for buf, t in ((kbuf, k), (vbuf, v)):
        t_sw = pltpu.roll(t, HEAD_DIM, 1)
        variants = (jnp.where(first_half, t, 0.0), jnp.where(first_half, 0.0, t_sw),
                    jnp.where(first_half, t_sw, 0.0), jnp.where(first_half, 0.0, t))
        for idx, var in enumerate(variants):
            buf[idx, BLOCK:BLOCK + T, :] = var.astype(_BF16)

    @pl.when((pl.program_id(0) == 0) & (s_idx == 0))
    def _():
        qi = lax.broadcasted_iota(jnp.int32, (2 * BLOCK, 2 * BLOCK), 0) % BLOCK
        kj = lax.broadcasted_iota(jnp.int32, (2 * BLOCK, 2 * BLOCK), 1)
        band = (kj > qi) & (kj <= qi + BLOCK)
        band_ref[0] = band.astype(_F32)
        band_ref[1] = (band & (kj >= BLOCK)).astype(_F32)

    upper_rows = lax.broadcasted_iota(jnp.int32, (2 * BLOCK, 1), 0) < BLOCK
    nt = (((1,), (1,)), ((), ()))
    n_units = N_KV_HEADS * T // BLOCK

    def rows(u, n):
        start = (u // N_KV_HEADS) * BLOCK
        return pl.ds(start if isinstance(u, int) else pl.multiple_of(start, BLOCK), n)

    def scores(u):
        kv, slot = u % N_KV_HEADS, u % 2
        q2 = jnp.concatenate([qbuf[2 * kv, rows(u, BLOCK), :], qbuf[2 * kv + 1, rows(u, BLOCK), :]], axis=0)
        first = (s_idx == 0) & (u < N_KV_HEADS)
        valid = band_ref[jnp.where(first, 1, 0)] > 0.5
        for e in range(2):
            s = lax.dot_general(q2, kbuf[2 * kv + e, rows(u, 2 * BLOCK), :], nt, preferred_element_type=_F32)
            sbuf[slot, e] = jnp.where(valid, s, NEG_INF)

    def softmax(u):
        kv, slot = u % N_KV_HEADS, u % 2
        for e in range(2):
            head_a = 4 * kv + e
            sink = jnp.where(upper_rows, sinks_ref[head_a] * LOG2E, sinks_ref[head_a + 2] * LOG2E)
            s = sbuf[slot, e]
            m = jnp.maximum(jnp.max(s, axis=1, keepdims=True), sink)
            pbuf[slot, e] = jnp.exp2(s - m).astype(_BF16)
            ibuf[slot, e] = jnp.broadcast_to(jnp.exp2(sink - m), (2 * BLOCK, LANES))

    ones = jnp.ones((2 * BLOCK, LANES), _BF16)

    def weighted(u):
        kv, slot = u % N_KV_HEADS, u % 2
        out2 = None
        for e in range(2):
            v_ones = jnp.concatenate([vbuf[2 * kv + e, rows(u, 2 * BLOCK), :], ones], axis=1)
            pv = jnp.dot(pbuf[slot, e], v_ones, preferred_element_type=_F32)
            pv = pv[:, :LANES] / (pv[:, LANES:] + ibuf[slot, e])
            out2 = pv if out2 is None else out2 + pv
        attn_buf[2 * kv, rows(u, BLOCK), :] = out2[:BLOCK]
        attn_buf[2 * kv + 1, rows(u, BLOCK), :] = out2[BLOCK:]

    scores(0)
    scores(1)
    softmax(0)

    def steady(u, carry):
        weighted(u - 2)
        softmax(u - 1)
        scores(u)
        return carry

    for u in range(2, n_units):
        steady(u, 0)
    softmax(n_units - 1)
    weighted(n_units - 2)
    weighted(n_units - 1)

    y_pool = _rms(pool_out, gpool_ref[...]).astype(_BF16)
    attn = jnp.concatenate([attn_buf[p] for p in range(ATTN_WIDTH // LANES)], axis=1)
    y_attn = _rms(attn, gattn_ref[...]).astype(_BF16)
    hout = (jnp.dot(y_pool, wout_ref[0:POOL_WIDTH, :], preferred_element_type=_F32)
            + jnp.dot(y_attn, wout_ref[POOL_WIDTH:, :], preferred_element_type=_F32))
    o_ref[0] = x + _rms(hout, post_ref[...])


def _mixer(x, positions, mix_pre, w_in, w_pool, pool_scale, sinks, g_pool, g_attn, w_out, mix_post):
    B, S, _ = x.shape
    T = MIX_TOKENS
    half = ROT_DIM // 2
    inv_freq = ROPE_THETA ** (-jnp.arange(0, ROT_DIM, 2, dtype=_F32) / ROT_DIM)
    dim = jnp.arange(LANES) % HEAD_DIM
    freq = jnp.arange(half)[:, None]
    cos_place = ((dim[None, :] < ROT_DIM) & (dim[None, :] % half == freq)).astype(_F32)
    sin_place = cos_place * jnp.where(dim < half, -1.0, 1.0)[None, :]
    zeros = jnp.zeros_like(cos_place)
    place = jnp.concatenate([jnp.concatenate([cos_place, zeros], axis=1),
                             jnp.concatenate([zeros, sin_place], axis=1)], axis=0)
    place = jnp.tile(place, (3, 1)).astype(_BF16)
    const = lambda nd: (lambda b, s, sinks_ref: (0,) * nd)
    row = lambda n: pl.BlockSpec((1, n), const(2))
    grid_spec = pltpu.PrefetchScalarGridSpec(
        num_scalar_prefetch=1,
        grid=(B, S // T),
        in_specs=[
            pl.BlockSpec((1, T, D_MODEL), lambda b, s, sinks_ref: (b, s, 0)),
            pl.BlockSpec((1, 1, T), lambda b, s, sinks_ref: (b, 0, s)),
            pl.BlockSpec((ROT_DIM // 2, 1), const(2)),
            pl.BlockSpec((3 * ROT_DIM, 2 * LANES), const(2)),
            row(D_MODEL),
            pl.BlockSpec((D_MODEL, IN_WIDTH), const(2)),
            pl.BlockSpec((len(POOL_WINDOWS), POOL_GROUP, POOL_GROUP), const(3)),
            row(POOL_WIDTH),
            row(POOL_WIDTH),
            row(ATTN_WIDTH),
            pl.BlockSpec((MIX_WIDTH, D_MODEL), const(2)),
            row(D_MODEL),
        ],
        out_specs=pl.BlockSpec((1, T, D_MODEL), lambda b, s, sinks_ref: (b, s, 0)),
        scratch_shapes=[
            pltpu.VMEM((T + POOL_HALO, POOL_WIDTH), _F32),
            pltpu.VMEM((4, T + BLOCK, LANES), _BF16),
            pltpu.VMEM((4, T + BLOCK, LANES), _BF16),
            pltpu.VMEM((ATTN_WIDTH // LANES, T, LANES), _BF16),
            pltpu.VMEM((ATTN_WIDTH // LANES, T, LANES), _F32),
            pltpu.VMEM((2, 2 * BLOCK, 2 * BLOCK), _F32),
            pltpu.VMEM((2, 2, 2 * BLOCK, 2 * BLOCK), _F32),
            pltpu.VMEM((2, 2, 2 * BLOCK, 2 * BLOCK), _BF16),
            pltpu.VMEM((2, 2, 2 * BLOCK, LANES), _F32),
        ],
    )
    return pl.pallas_call(
        _mixer_kernel,
        name="mixer",
        grid_spec=grid_spec,
        out_shape=jax.ShapeDtypeStruct(x.shape, _F32),
        compiler_params=pltpu.CompilerParams(
            dimension_semantics=("arbitrary", "arbitrary"), vmem_limit_bytes=MIX_VMEM_BYTES),
    )(sinks.astype(_F32), x, positions.reshape(B, 1, S), inv_freq.reshape(half, 1), place, mix_pre.reshape(1, D_MODEL),
      w_in.astype(_BF16), w_pool.astype(_BF16), pool_scale.reshape(1, POOL_WIDTH),
      g_pool.reshape(1, POOL_WIDTH), g_attn.reshape(1, ATTN_WIDTH), w_out.astype(_BF16),
      mix_post.reshape(1, D_MODEL))


def kernel(x, positions, ffn1_pre, ffn1_w_gu, ffn1_w_down, ffn1_post, mix_pre, w_in, w_pool, pool_scale, sinks, g_pool, g_attn, w_out, mix_post, ffn2_pre, ffn2_w_gu, ffn2_w_down, ffn2_post):
    B, S, D = x.shape
    for l in range(ffn1_pre.shape[0]):
        x = _ffn(x.reshape(B * S, D), ffn1_pre[l], ffn1_w_gu[l], ffn1_w_down[l], ffn1_post[l]).reshape(B, S, D)
        x = _mixer(x, positions, mix_pre[l], w_in[l], w_pool[l], pool_scale[l], sinks[l], g_pool[l],
                   g_attn[l], w_out[l], mix_post[l])
        x = _ffn(x.reshape(B * S, D), ffn2_pre[l], ffn2_w_gu[l], ffn2_w_down[l], ffn2_post[l]).reshape(B, S, D)
    return x
```

```python
import collections
import functools

import jax
import jax.numpy as jnp
from jax import lax
from jax.experimental import pallas as pl
from jax.experimental.pallas import tpu as pltpu

D_MODEL = 1024
D_FF = 2816
POOL_WINDOWS = (2, 4, 8, 16)
POOL_WIDTH = 512
POOL_GROUP = 128
HEAD_DIM = 64
N_HEADS = 8
N_KV_HEADS = 2
ATTN_WIDTH = 512
KV_WIDTH = 128
BLOCK = 128
ROPE_THETA = 500000.0
ROT_DIM = 16
MIX_WIDTH = 1024
IN_WIDTH = 1280
EPS = 1e-6
NEG_INF = -1e30
LOG2E = 1.4426950408889634

LANES = 128
POOL_HALO = 16

FFN_TOKENS = 512
FFN_CHUNK = 256
MIX_TOKENS = 512
FFN_VMEM_BYTES = 48 * 1024 * 1024
MIX_VMEM_BYTES = 56 * 1024 * 1024

_F32 = jnp.float32
_BF16 = jnp.bfloat16


def _rms(x, g):
    return x * lax.rsqrt(jnp.mean(x * x, axis=-1, keepdims=True) + EPS) * g


def _ffn_kernel(x_ref, pre_ref, wgu_ref, wd_ref, post_ref, o_ref, xn_ref, act_ref):
    x = x_ref[...]
    xn_ref[...] = _rms(x, pre_ref[...]).astype(_BF16)
    for c in range(D_FF // FFN_CHUNK):
        lo = c * FFN_CHUNK
        xn = xn_ref[...]
        g = jnp.dot(xn, wgu_ref[:, lo:lo + FFN_CHUNK], preferred_element_type=_F32)
        u = jnp.dot(xn, wgu_ref[:, D_FF + lo:D_FF + lo + FFN_CHUNK], preferred_element_type=_F32)
        act_ref[:, lo:lo + FFN_CHUNK] = (g * jax.nn.sigmoid(g) * u).astype(_BF16)
    h = jnp.dot(act_ref[...], wd_ref[...], preferred_element_type=_F32)
    o_ref[...] = x + 0.5 * _rms(h, post_ref[...])


def _ffn(x2d, pre, w_gu, w_down, post):
    n_tok = x2d.shape[0]
    const = lambda i: (0, 0)
    resident = dict(pipeline_mode=pl.Buffered(1))
    return pl.pallas_call(
        _ffn_kernel,
        name="ffn",
        grid=(n_tok // FFN_TOKENS,),
        in_specs=[
            pl.BlockSpec((FFN_TOKENS, D_MODEL), lambda i: (i, 0)),
            pl.BlockSpec((1, D_MODEL), const, **resident),
            pl.BlockSpec((D_MODEL, 2 * D_FF), const, **resident),
            pl.BlockSpec((D_FF, D_MODEL), const, **resident),
            pl.BlockSpec((1, D_MODEL), const, **resident),
        ],
        out_specs=pl.BlockSpec((FFN_TOKENS, D_MODEL), lambda i: (i, 0)),
        out_shape=jax.ShapeDtypeStruct(x2d.shape, _F32),
        scratch_shapes=[
            pltpu.VMEM((FFN_TOKENS, D_MODEL), _BF16),
            pltpu.VMEM((FFN_TOKENS, D_FF), _BF16),
        ],
        compiler_params=pltpu.CompilerParams(
            dimension_semantics=("arbitrary",), vmem_limit_bytes=FFN_VMEM_BYTES),
    )(x2d, pre.reshape(1, D_MODEL), w_gu.astype(_BF16), w_down.astype(_BF16), post.reshape(1, D_MODEL))


_TileScratch = collections.namedtuple("_TileScratch", "ubuf kbuf vbuf qbuf attn sbuf pbuf ibuf")
_Params = collections.namedtuple(
    "_Params", "sinks invf place pre win wpool pscale gpool gattn wout post band")


def _tile_scratch_shapes(T):
    return [
        pltpu.VMEM((T + POOL_HALO, POOL_WIDTH), _F32),
        pltpu.VMEM((4, T + BLOCK, LANES), _BF16),
        pltpu.VMEM((4, T + BLOCK, LANES), _BF16),
        pltpu.VMEM((ATTN_WIDTH // LANES, T, LANES), _BF16),
        pltpu.VMEM((ATTN_WIDTH // LANES, T, LANES), _F32),
        pltpu.VMEM((2, 2, 2 * BLOCK, 2 * BLOCK), _F32),
        pltpu.VMEM((2, 2, 2 * BLOCK, 2 * BLOCK), _BF16),
        pltpu.VMEM((2, 2, 2 * BLOCK, LANES), _F32),
    ]


def _rope(t, cos_t, sin_t, lo_lanes):
    partner = jnp.where(lo_lanes, pltpu.roll(t, LANES - ROT_DIM // 2, 1), pltpu.roll(t, ROT_DIM // 2, 1))
    return t * cos_t + partner * sin_t


def _project(x, pos_row, prm, cur, prev, is_start):
    T = x.shape[0]
    h = _rms(x, prm.pre[...]).astype(_BF16)
    z = jnp.dot(h, prm.win[...], preferred_element_type=_F32)

    if prev is None:
        cur.ubuf[0:POOL_HALO, :] = jnp.zeros((POOL_HALO, POOL_WIDTH), _F32)
        cur.kbuf[:, 0:BLOCK, :] = jnp.zeros((4, BLOCK, LANES), _BF16)
        cur.vbuf[:, 0:BLOCK, :] = jnp.zeros((4, BLOCK, LANES), _BF16)
    else:
        cur.ubuf[0:POOL_HALO, :] = jnp.where(is_start, 0.0, prev.ubuf[T:T + POOL_HALO, :])
        cur.kbuf[:, 0:BLOCK, :] = jnp.where(is_start, 0.0, prev.kbuf[:, T:T + BLOCK, :]).astype(_BF16)
        cur.vbuf[:, 0:BLOCK, :] = jnp.where(is_start, 0.0, prev.vbuf[:, T:T + BLOCK, :]).astype(_BF16)
    cur.ubuf[POOL_HALO:POOL_HALO + T, :] = z[:, :POOL_WIDTH]

    lane = lax.broadcasted_iota(jnp.int32, (1, LANES), 1)
    dim = lane % HEAD_DIM
    lo_lanes = dim < ROT_DIM // 2
    ang = pos_row.astype(_F32) * prm.invf[...]
    cs = jnp.concatenate([jnp.cos(ang), jnp.sin(ang)], axis=0)
    cs_hi = cs.astype(_BF16)
    rem = cs - cs_hi.astype(_F32)
    cs_mid = rem.astype(_BF16)
    cs_lo = (rem - cs_mid.astype(_F32)).astype(_BF16)
    cs3 = jnp.concatenate([cs_hi, cs_mid, cs_lo], axis=0)
    tables = lax.dot_general(cs3, prm.place[...], (((0,), (0,)), ((), ())), preferred_element_type=_F32)
    cos_t = jnp.where(dim < ROT_DIM, tables[:, :LANES], 1.0)
    sin_t = tables[:, LANES:]

    scale = HEAD_DIM ** -0.5 * LOG2E
    for p in range(ATTN_WIDTH // LANES):
        qp = z[:, POOL_WIDTH + p * LANES:POOL_WIDTH + (p + 1) * LANES]
        cur.qbuf[p] = (_rope(qp, cos_t, sin_t, lo_lanes) * scale).astype(_BF16)

    first_half = lane < HEAD_DIM
    k = _rope(z[:, MIX_WIDTH:MIX_WIDTH + KV_WIDTH], cos_t, sin_t, lo_lanes)
    v = z[:, MIX_WIDTH + KV_WIDTH:]
    for buf, t in ((cur.kbuf, k), (cur.vbuf, v)):
        t_sw = pltpu.roll(t, HEAD_DIM, 1)
        variants = (jnp.where(first_half, t, 0.0), jnp.where(first_half, 0.0, t_sw),
                    jnp.where(first_half, t_sw, 0.0), jnp.where(first_half, 0.0, t))
        for idx, var in enumerate(variants):
            buf[idx, BLOCK:BLOCK + T, :] = var.astype(_BF16)


def _mix(x, prm, cur, s_tile):
    T = x.shape[0]
    is_start = s_tile == 0

    row = lax.broadcasted_iota(jnp.int32, (T, 1), 0)
    seen = s_tile * T + row + 1
    pooled = []
    for g, w in enumerate(POOL_WINDOWS):
        lo = g * POOL_GROUP
        win_sum = cur.ubuf[:, lo:lo + POOL_GROUP]
        span = 1
        while span < w:
            win_sum = win_sum + pltpu.roll(win_sum, span, 0)
            span *= 2
        cnt = jnp.minimum(seen, w).astype(_F32)
        d = win_sum[POOL_HALO:, :] / cnt - cur.ubuf[POOL_HALO:POOL_HALO + T, lo:lo + POOL_GROUP]
        pooled.append(jnp.dot(d.astype(_BF16), prm.wpool[g], preferred_element_type=_F32))
    pool_out = jnp.concatenate(pooled, axis=1) * prm.pscale[...]

    upper_rows = lax.broadcasted_iota(jnp.int32, (2 * BLOCK, 1), 0) < BLOCK
    nt = (((1,), (1,)), ((), ()))
    n_units = N_KV_HEADS * T // BLOCK
    ones = jnp.ones((2 * BLOCK, LANES), _BF16)

    def rows(u, n):
        return pl.ds((u // N_KV_HEADS) * BLOCK, n)

    def scores(u):
        kv, slot = u % N_KV_HEADS, u % 2
        q2 = jnp.concatenate([cur.qbuf[2 * kv, rows(u, BLOCK), :], cur.qbuf[2 * kv + 1, rows(u, BLOCK), :]],
                             axis=0)
        valid = prm.band[jnp.where(is_start, 1, 0) if u < N_KV_HEADS else 0] > 0.5
        for e in range(2):
            s = lax.dot_general(q2, cur.kbuf[2 * kv + e, rows(u, 2 * BLOCK), :], nt, preferred_element_type=_F32)
            cur.sbuf[slot, e] = jnp.where(valid, s, NEG_INF)

    def softmax(u):
        kv, slot = u % N_KV_HEADS, u % 2
        for e in range(2):
            head_a = 4 * kv + e
            sink = jnp.where(upper_rows, prm.sinks[head_a] * LOG2E, prm.sinks[head_a + 2] * LOG2E)
            s = cur.sbuf[slot, e]
            m = jnp.maximum(jnp.max(s, axis=1, keepdims=True), sink)
            cur.pbuf[slot, e] = jnp.exp2(s - m).astype(_BF16)
            cur.ibuf[slot, e] = jnp.broadcast_to(jnp.exp2(sink - m), (2 * BLOCK, LANES))

    def weighted(u):
        kv, slot = u % N_KV_HEADS, u % 2
        out2 = None
        for e in range(2):
            v_ones = jnp.concatenate([cur.vbuf[2 * kv + e, rows(u, 2 * BLOCK), :], ones], axis=1)
            pv = jnp.dot(cur.pbuf[slot, e], v_ones, preferred_element_type=_F32)
            pv = pv[:, :LANES] / (pv[:, LANES:] + cur.ibuf[slot, e])
            out2 = pv if out2 is None else out2 + pv
        cur.attn[2 * kv, rows(u, BLOCK), :] = out2[:BLOCK]
        cur.attn[2 * kv + 1, rows(u, BLOCK), :] = out2[BLOCK:]

    scores(0)
    scores(1)
    softmax(0)
    for u in range(2, n_units):
        weighted(u - 2)
        softmax(u - 1)
        scores(u)
    softmax(n_units - 1)
    weighted(n_units - 2)
    weighted(n_units - 1)

    y_pool = _rms(pool_out, prm.gpool[...]).astype(_BF16)
    attn = jnp.concatenate([cur.attn[p] for p in range(ATTN_WIDTH // LANES)], axis=1)
    y_attn = _rms(attn, prm.gattn[...]).astype(_BF16)
    hout = (jnp.dot(y_pool, prm.wout[0:POOL_WIDTH, :], preferred_element_type=_F32)
            + jnp.dot(y_attn, prm.wout[POOL_WIDTH:, :], preferred_element_type=_F32))
    return x + _rms(hout, prm.post[...])


def _mixer_kernel(tiles_per_seq, sinks_ref, x_ref, xn_ref, pos_ref, posn_ref, invf_ref, place_ref, pre_ref,
                  win_ref, wpool_ref, pscale_ref, gpool_ref, gattn_ref, wout_ref, post_ref, o_ref,
                  band_ref, *scratch):
    T = MIX_TOKENS
    n_set = len(scratch) // 2
    set0, set1 = _TileScratch(*scratch[:n_set]), _TileScratch(*scratch[n_set:])
    prm = _Params(sinks_ref, invf_ref, place_ref, pre_ref, win_ref, wpool_ref, pscale_ref, gpool_ref,
                  gattn_ref, wout_ref, post_ref, band_ref)
    g = pl.program_id(0)
    s_even = (2 * g) % tiles_per_seq

    @pl.when(g == 0)
    def _():
        qi = lax.broadcasted_iota(jnp.int32, (2 * BLOCK, 2 * BLOCK), 0) % BLOCK
        kj = lax.broadcasted_iota(jnp.int32, (2 * BLOCK, 2 * BLOCK), 1)
        band = (kj > qi) & (kj <= qi + BLOCK)
        band_ref[0] = band.astype(_F32)
        band_ref[1] = (band & (kj >= BLOCK)).astype(_F32)
        _project(x_ref[0:T, :], pos_ref[:, 0:T], prm, set0, None, True)

    _project(x_ref[T:2 * T, :], pos_ref[:, T:2 * T], prm, set1, set0, False)
    o_ref[0:T, :] = _mix(x_ref[0:T, :], prm, set0, s_even)

    next_start = (2 * g + 2) % tiles_per_seq == 0
    _project(xn_ref[...], posn_ref[...], prm, set0, set1, next_start)
    o_ref[T:2 * T, :] = _mix(x_ref[T:2 * T, :], prm, set1, s_even + 1)


def _mixer(x, positions, mix_pre, w_in, w_pool, pool_scale, sinks, g_pool, g_attn, w_out, mix_post):
    B, S, _ = x.shape
    T = MIX_TOKENS
    n_tiles = B * S // T
    assert S % (2 * T) == 0
    half = ROT_DIM // 2
    inv_freq = ROPE_THETA ** (-jnp.arange(0, ROT_DIM, 2, dtype=_F32) / ROT_DIM)
    dim = jnp.arange(LANES) % HEAD_DIM
    freq = jnp.arange(half)[:, None]
    cos_place = ((dim[None, :] < ROT_DIM) & (dim[None, :] % half == freq)).astype(_F32)
    sin_place = cos_place * jnp.where(dim < half, -1.0, 1.0)[None, :]
    zeros = jnp.zeros_like(cos_place)
    place = jnp.concatenate([jnp.concatenate([cos_place, zeros], axis=1),
                             jnp.concatenate([zeros, sin_place], axis=1)], axis=0)
    place = jnp.tile(place, (3, 1)).astype(_BF16)

    resident = dict(pipeline_mode=pl.Buffered(1))
    full = lambda shape: pl.BlockSpec(shape, lambda g, sinks_ref: (0,) * len(shape), **resident)
    next_tile = lambda g, sinks_ref: jnp.minimum(2 * g + 2, n_tiles - 1)
    grid_spec = pltpu.PrefetchScalarGridSpec(
        num_scalar_prefetch=1,
        grid=(n_tiles // 2,),
        in_specs=[
            pl.BlockSpec((2 * T, D_MODEL), lambda g, sinks_ref: (g, 0)),
            pl.BlockSpec((T, D_MODEL), lambda g, sinks_ref: (next_tile(g, sinks_ref), 0)),
            pl.BlockSpec((1, 2 * T), lambda g, sinks_ref: (0, g)),
            pl.BlockSpec((1, T), lambda g, sinks_ref: (0, next_tile(g, sinks_ref))),
            full((half, 1)),
            full((3 * ROT_DIM, 2 * LANES)),
            full((1, D_MODEL)),
            full((D_MODEL, IN_WIDTH)),
            full((len(POOL_WINDOWS), POOL_GROUP, POOL_GROUP)),
            full((1, POOL_WIDTH)),
            full((1, POOL_WIDTH)),
            full((1, ATTN_WIDTH)),
            full((MIX_WIDTH, D_MODEL)),
            full((1, D_MODEL)),
        ],
        out_specs=pl.BlockSpec((2 * T, D_MODEL), lambda g, sinks_ref: (g, 0)),
        scratch_shapes=[pltpu.VMEM((2, 2 * BLOCK, 2 * BLOCK), _F32)]
        + _tile_scratch_shapes(T) + _tile_scratch_shapes(T),
    )
    out = pl.pallas_call(
        functools.partial(_mixer_kernel, S // T),
        name="mixer",
        grid_spec=grid_spec,
        out_shape=jax.ShapeDtypeStruct((B * S, D_MODEL), _F32),
        compiler_params=pltpu.CompilerParams(
            dimension_semantics=("arbitrary",), vmem_limit_bytes=MIX_VMEM_BYTES),
    )(sinks.astype(_F32), x.reshape(B * S, D_MODEL), x.reshape(B * S, D_MODEL),
      positions.reshape(1, B * S), positions.reshape(1, B * S), inv_freq.reshape(half, 1), place,
      mix_pre.reshape(1, D_MODEL), w_in.astype(_BF16), w_pool.astype(_BF16), pool_scale.reshape(1, POOL_WIDTH),
      g_pool.reshape(1, POOL_WIDTH), g_attn.reshape(1, ATTN_WIDTH), w_out.astype(_BF16),
      mix_post.reshape(1, D_MODEL))
    return out.reshape(B, S, D_MODEL)


def kernel(x, positions, ffn1_pre, ffn1_w_gu, ffn1_w_down, ffn1_post, mix_pre, w_in, w_pool, pool_scale, sinks, g_pool, g_attn, w_out, mix_post, ffn2_pre, ffn2_w_gu, ffn2_w_down, ffn2_post):
    B, S, D = x.shape
    for l in range(ffn1_pre.shape[0]):
        x = _ffn(x.reshape(B * S, D), ffn1_pre[l], ffn1_w_gu[l], ffn1_w_down[l], ffn1_post[l]).reshape(B, S, D)
        x = _mixer(x, positions, mix_pre[l], w_in[l], w_pool[l], pool_scale[l], sinks[l], g_pool[l],
                   g_attn[l], w_out[l], mix_post[l])
        x = _ffn(x.reshape(B * S, D), ffn2_pre[l], ffn2_w_gu[l], ffn2_w_down[l], ffn2_post[l]).reshape(B, S, D)
    return x
```

```python
import collections
import functools

import jax
import jax.numpy as jnp
from jax import lax
from jax.experimental import pallas as pl
from jax.experimental.pallas import tpu as pltpu

D_MODEL = 1024
D_FF = 2816
POOL_WINDOWS = (2, 4, 8, 16)
POOL_WIDTH = 512
POOL_GROUP = 128
HEAD_DIM = 64
N_HEADS = 8
N_KV_HEADS = 2
ATTN_WIDTH = 512
KV_WIDTH = 128
BLOCK = 128
ROPE_THETA = 500000.0
ROT_DIM = 16
MIX_WIDTH = 1024
IN_WIDTH = 1280
EPS = 1e-6
NEG_INF = -1e30
LOG2E = 1.4426950408889634

LANES = 128
POOL_HALO = 16

FFN_TOKENS = 512
FFN_SUBTILES = 2
FFN_CHUNK = 256
MIX_TOKENS = 512
FFN_VMEM_BYTES = 56 * 1024 * 1024
MIX_VMEM_BYTES = 56 * 1024 * 1024

_F32 = jnp.float32
_BF16 = jnp.bfloat16


def _rms(x, g):
    return x * lax.rsqrt(jnp.mean(x * x, axis=-1, keepdims=True) + EPS) * g


def _ffn_kernel(x_ref, pre_ref, wgu_ref, wd_ref, post_ref, o_ref, xn_ref, act_ref):
    T = FFN_TOKENS
    for t in range(FFN_SUBTILES):
        x = x_ref[t * T:(t + 1) * T, :]
        xn_ref[...] = _rms(x, pre_ref[...]).astype(_BF16)
        for c in range(D_FF // FFN_CHUNK):
            lo = c * FFN_CHUNK
            xn = xn_ref[...]
            g = jnp.dot(xn, wgu_ref[:, lo:lo + FFN_CHUNK], preferred_element_type=_F32)
            u = jnp.dot(xn, wgu_ref[:, D_FF + lo:D_FF + lo + FFN_CHUNK], preferred_element_type=_F32)
            act_ref[:, lo:lo + FFN_CHUNK] = (g * jax.nn.sigmoid(g) * u).astype(_BF16)
        h = jnp.dot(act_ref[...], wd_ref[...], preferred_element_type=_F32)
        o_ref[t * T:(t + 1) * T, :] = x + _rms(h, post_ref[...])


def _ffn(x2d, pre, w_gu, w_down, post):
    n_tok = x2d.shape[0]
    step_tokens = FFN_SUBTILES * FFN_TOKENS
    n_steps = n_tok // step_tokens
    const = lambda i: (0, 0)
    resident = dict(pipeline_mode=pl.Buffered(1))
    return pl.pallas_call(
        _ffn_kernel,
        name="ffn",
        grid=(n_steps,),
        in_specs=[
            pl.BlockSpec((step_tokens, D_MODEL), lambda i: (i, 0)),
            pl.BlockSpec((1, D_MODEL), const, **resident),
            pl.BlockSpec((D_MODEL, 2 * D_FF), const, **resident),
            pl.BlockSpec((D_FF, D_MODEL), const, **resident),
            pl.BlockSpec((1, D_MODEL), const, **resident),
        ],
        out_specs=pl.BlockSpec((step_tokens, D_MODEL), lambda i: (i, 0)),
        out_shape=jax.ShapeDtypeStruct(x2d.shape, _F32),
        scratch_shapes=[pltpu.VMEM((FFN_TOKENS, D_MODEL), _BF16), pltpu.VMEM((FFN_TOKENS, D_FF), _BF16)],
        compiler_params=pltpu.CompilerParams(
            dimension_semantics=("arbitrary",), vmem_limit_bytes=FFN_VMEM_BYTES),
    )(x2d, pre.reshape(1, D_MODEL), w_gu.astype(_BF16), w_down.astype(_BF16), (0.5 * post).reshape(1, D_MODEL))


_TileScratch = collections.namedtuple("_TileScratch", "ubuf kbuf vbuf qbuf attn sbuf pbuf ibuf")
_Params = collections.namedtuple(
    "_Params", "sinks invf place pre win wpool pscale gpool gattn wout post band")


def _tile_scratch_shapes(T):
    return [
        pltpu.VMEM((T + POOL_HALO, POOL_WIDTH), _F32),
        pltpu.VMEM((4, T + BLOCK, LANES), _BF16),
        pltpu.VMEM((4, T + BLOCK, LANES), _BF16),
        pltpu.VMEM((ATTN_WIDTH // LANES, T, LANES), _BF16),
        pltpu.VMEM((ATTN_WIDTH // LANES, T, LANES), _F32),
        pltpu.VMEM((2, 2, 2 * BLOCK, 2 * BLOCK), _F32),
        pltpu.VMEM((2, 2, 2 * BLOCK, 2 * BLOCK), _BF16),
        pltpu.VMEM((2, 2, 2 * BLOCK, LANES), _F32),
    ]


def _rope(t, cos_t, sin_t, lo_lanes):
    partner = jnp.where(lo_lanes, pltpu.roll(t, LANES - ROT_DIM // 2, 1), pltpu.roll(t, ROT_DIM // 2, 1))
    return t * cos_t + partner * sin_t


def _project(x, pos_row, prm, cur, prev, is_start):
    T = x.shape[0]
    h = _rms(x, prm.pre[...]).astype(_BF16)
    z = jnp.dot(h, prm.win[...], preferred_element_type=_F32)

    if prev is None:
        cur.ubuf[0:POOL_HALO, :] = jnp.zeros((POOL_HALO, POOL_WIDTH), _F32)
        cur.kbuf[:, 0:BLOCK, :] = jnp.zeros((4, BLOCK, LANES), _BF16)
        cur.vbuf[:, 0:BLOCK, :] = jnp.zeros((4, BLOCK, LANES), _BF16)
    else:
        cur.ubuf[0:POOL_HALO, :] = jnp.where(is_start, 0.0, prev.ubuf[T:T + POOL_HALO, :])
        cur.kbuf[:, 0:BLOCK, :] = jnp.where(is_start, 0.0, prev.kbuf[:, T:T + BLOCK, :]).astype(_BF16)
        cur.vbuf[:, 0:BLOCK, :] = jnp.where(is_start, 0.0, prev.vbuf[:, T:T + BLOCK, :]).astype(_BF16)
    cur.ubuf[POOL_HALO:POOL_HALO + T, :] = z[:, :POOL_WIDTH]

    lane = lax.broadcasted_iota(jnp.int32, (1, LANES), 1)
    dim = lane % HEAD_DIM
    lo_lanes = dim < ROT_DIM // 2
    ang = pos_row.astype(_F32) * prm.invf[...]
    cs = jnp.concatenate([jnp.cos(ang), jnp.sin(ang)], axis=0)
    cs_hi = cs.astype(_BF16)
    rem = cs - cs_hi.astype(_F32)
    cs_mid = rem.astype(_BF16)
    cs_lo = (rem - cs_mid.astype(_F32)).astype(_BF16)
    cs3 = jnp.concatenate([cs_hi, cs_mid, cs_lo], axis=0)
    tables = lax.dot_general(cs3, prm.place[...], (((0,), (0,)), ((), ())), preferred_element_type=_F32)
    cos_t = jnp.where(dim < ROT_DIM, tables[:, :LANES], 1.0)
    sin_t = tables[:, LANES:]

    scale = HEAD_DIM ** -0.5 * LOG2E
    for p in range(ATTN_WIDTH // LANES):
        qp = z[:, POOL_WIDTH + p * LANES:POOL_WIDTH + (p + 1) * LANES]
        cur.qbuf[p] = (_rope(qp, cos_t, sin_t, lo_lanes) * scale).astype(_BF16)

    first_half = lane < HEAD_DIM
    k = _rope(z[:, MIX_WIDTH:MIX_WIDTH + KV_WIDTH], cos_t, sin_t, lo_lanes)
    v = z[:, MIX_WIDTH + KV_WIDTH:]
    for buf, t in ((cur.kbuf, k), (cur.vbuf, v)):
        t_sw = pltpu.roll(t, HEAD_DIM, 1)
        variants = (jnp.where(first_half, t, 0.0), jnp.where(first_half, 0.0, t_sw),
                    jnp.where(first_half, t_sw, 0.0), jnp.where(first_half, 0.0, t))
        for idx, var in enumerate(variants):
            buf[idx, BLOCK:BLOCK + T, :] = var.astype(_BF16)


def _mix(x, prm, cur, s_tile):
    T = x.shape[0]
    is_start = s_tile == 0

    row = lax.broadcasted_iota(jnp.int32, (T, 1), 0)
    seen = s_tile * T + row + 1
    pooled = []
    for g, w in enumerate(POOL_WINDOWS):
        lo = g * POOL_GROUP
        win_sum = cur.ubuf[:, lo:lo + POOL_GROUP]
        span = 1
        while span < w:
            win_sum = win_sum + pltpu.roll(win_sum, span, 0)
            span *= 2
        cnt = jnp.minimum(seen, w).astype(_F32)
        d = win_sum[POOL_HALO:, :] / cnt - cur.ubuf[POOL_HALO:POOL_HALO + T, lo:lo + POOL_GROUP]
        pooled.append(jnp.dot(d.astype(_BF16), prm.wpool[g], preferred_element_type=_F32))
    pool_out = jnp.concatenate(pooled, axis=1) * prm.pscale[...]

    upper_rows = lax.broadcasted_iota(jnp.int32, (2 * BLOCK, 1), 0) < BLOCK
    nt = (((1,), (1,)), ((), ()))
    n_units = N_KV_HEADS * T // BLOCK
    ones = jnp.ones((2 * BLOCK, LANES), _BF16)

    def rows(u, n):
        return pl.ds((u // N_KV_HEADS) * BLOCK, n)

    def scores(u):
        kv, slot = u % N_KV_HEADS, u % 2
        q2 = jnp.concatenate([cur.qbuf[2 * kv, rows(u, BLOCK), :], cur.qbuf[2 * kv + 1, rows(u, BLOCK), :]],
                             axis=0)
        valid = prm.band[jnp.where(is_start, 1, 0) if u < N_KV_HEADS else 0] > 0.5
        for e in range(2):
            s = lax.dot_general(q2, cur.kbuf[2 * kv + e, rows(u, 2 * BLOCK), :], nt, preferred_element_type=_F32)
            cur.sbuf[slot, e] = jnp.where(valid, s, NEG_INF)

    def softmax(u):
        kv, slot = u % N_KV_HEADS, u % 2
        for e in range(2):
            head_a = 4 * kv + e
            sink = jnp.where(upper_rows, prm.sinks[head_a] * LOG2E, prm.sinks[head_a + 2] * LOG2E)
            s = cur.sbuf[slot, e]
            m = jnp.maximum(jnp.max(s, axis=1, keepdims=True), sink)
            cur.pbuf[slot, e] = jnp.exp2(s - m).astype(_BF16)
            cur.ibuf[slot, e] = jnp.broadcast_to(jnp.exp2(sink - m), (2 * BLOCK, LANES))

    def weighted(u):
        kv, slot = u % N_KV_HEADS, u % 2
        out2 = None
        for e in range(2):
            v_ones = jnp.concatenate([cur.vbuf[2 * kv + e, rows(u, 2 * BLOCK), :], ones], axis=1)
            pv = jnp.dot(cur.pbuf[slot, e], v_ones, preferred_element_type=_F32)
            pv = pv[:, :LANES] / (pv[:, LANES:] + cur.ibuf[slot, e])
            out2 = pv if out2 is None else out2 + pv
        cur.attn[2 * kv, rows(u, BLOCK), :] = out2[:BLOCK]
        cur.attn[2 * kv + 1, rows(u, BLOCK), :] = out2[BLOCK:]

    scores(0)
    scores(1)
    softmax(0)
    for u in range(2, n_units):
        weighted(u - 2)
        softmax(u - 1)
        scores(u)
    softmax(n_units - 1)
    weighted(n_units - 2)
    weighted(n_units - 1)

    y_pool = _rms(pool_out, prm.gpool[...]).astype(_BF16)
    attn = jnp.concatenate([cur.attn[p] for p in range(ATTN_WIDTH // LANES)], axis=1)
    y_attn = _rms(attn, prm.gattn[...]).astype(_BF16)
    hout = (jnp.dot(y_pool, prm.wout[0:POOL_WIDTH, :], preferred_element_type=_F32)
            + jnp.dot(y_attn, prm.wout[POOL_WIDTH:, :], preferred_element_type=_F32))
    return x + _rms(hout, prm.post[...])


def _mixer_kernel(tiles_per_seq, sinks_ref, x_ref, xn_ref, pos_ref, posn_ref, invf_ref, place_ref, pre_ref,
                  win_ref, wpool_ref, pscale_ref, gpool_ref, gattn_ref, wout_ref, post_ref, o_ref,
                  band_ref, *scratch):
    T = MIX_TOKENS
    n_set = len(scratch) // 2
    set0, set1 = _TileScratch(*scratch[:n_set]), _TileScratch(*scratch[n_set:])
    prm = _Params(sinks_ref, invf_ref, place_ref, pre_ref, win_ref, wpool_ref, pscale_ref, gpool_ref,
                  gattn_ref, wout_ref, post_ref, band_ref)
    g = pl.program_id(0)
    s_even = (2 * g) % tiles_per_seq

    @pl.when(g == 0)
    def _():
        qi = lax.broadcasted_iota(jnp.int32, (2 * BLOCK, 2 * BLOCK), 0) % BLOCK
        kj = lax.broadcasted_iota(jnp.int32, (2 * BLOCK, 2 * BLOCK), 1)
        band = (kj > qi) & (kj <= qi + BLOCK)
        band_ref[0] = band.astype(_F32)
        band_ref[1] = (band & (kj >= BLOCK)).astype(_F32)
        _project(x_ref[0:T, :], pos_ref[:, 0:T], prm, set0, None, True)

    _project(x_ref[T:2 * T, :], pos_ref[:, T:2 * T], prm, set1, set0, False)
    o_ref[0:T, :] = _mix(x_ref[0:T, :], prm, set0, s_even)

    next_start = (2 * g + 2) % tiles_per_seq == 0
    _project(xn_ref[...], posn_ref[...], prm, set0, set1, next_start)
    o_ref[T:2 * T, :] = _mix(x_ref[T:2 * T, :], prm, set1, s_even + 1)


def _mixer(x, positions, mix_pre, w_in, w_pool, pool_scale, sinks, g_pool, g_attn, w_out, mix_post):
    B, S, _ = x.shape
    T = MIX_TOKENS
    n_tiles = B * S // T
    assert S % (2 * T) == 0
    half = ROT_DIM // 2
    inv_freq = ROPE_THETA ** (-jnp.arange(0, ROT_DIM, 2, dtype=_F32) / ROT_DIM)
    dim = jnp.arange(LANES) % HEAD_DIM
    freq = jnp.arange(half)[:, None]
    cos_place = ((dim[None, :] < ROT_DIM) & (dim[None, :] % half == freq)).astype(_F32)
    sin_place = cos_place * jnp.where(dim < half, -1.0, 1.0)[None, :]
    zeros = jnp.zeros_like(cos_place)
    place = jnp.concatenate([jnp.concatenate([cos_place, zeros], axis=1),
                             jnp.concatenate([zeros, sin_place], axis=1)], axis=0)
    place = jnp.tile(place, (3, 1)).astype(_BF16)

    resident = dict(pipeline_mode=pl.Buffered(1))
    full = lambda shape: pl.BlockSpec(shape, lambda g, sinks_ref: (0,) * len(shape), **resident)
    next_tile = lambda g, sinks_ref: jnp.minimum(2 * g + 2, n_tiles - 1)
    grid_spec = pltpu.PrefetchScalarGridSpec(
        num_scalar_prefetch=1,
        grid=(n_tiles // 2,),
        in_specs=[
            pl.BlockSpec((2 * T, D_MODEL), lambda g, sinks_ref: (g, 0)),
            pl.BlockSpec((T, D_MODEL), lambda g, sinks_ref: (next_tile(g, sinks_ref), 0)),
            pl.BlockSpec((1, 2 * T), lambda g, sinks_ref: (0, g)),
            pl.BlockSpec((1, T), lambda g, sinks_ref: (0, next_tile(g, sinks_ref))),
            full((half, 1)),
            full((3 * ROT_DIM, 2 * LANES)),
            full((1, D_MODEL)),
            full((D_MODEL, IN_WIDTH)),
            full((len(POOL_WINDOWS), POOL_GROUP, POOL_GROUP)),
            full((1, POOL_WIDTH)),
            full((1, POOL_WIDTH)),
            full((1, ATTN_WIDTH)),
            full((MIX_WIDTH, D_MODEL)),
            full((1, D_MODEL)),
        ],
        out_specs=pl.BlockSpec((2 * T, D_MODEL), lambda g, sinks_ref: (g, 0)),
        scratch_shapes=[pltpu.VMEM((2, 2 * BLOCK, 2 * BLOCK), _F32)]
        + _tile_scratch_shapes(T) + _tile_scratch_shapes(T),
    )
    out = pl.pallas_call(
        functools.partial(_mixer_kernel, S // T),
        name="mixer",
        grid_spec=grid_spec,
        out_shape=jax.ShapeDtypeStruct((B * S, D_MODEL), _F32),
        compiler_params=pltpu.CompilerParams(
            dimension_semantics=("arbitrary",), vmem_limit_bytes=MIX_VMEM_BYTES),
    )(sinks.astype(_F32), x.reshape(B * S, D_MODEL), x.reshape(B * S, D_MODEL),
      positions.reshape(1, B * S), positions.reshape(1, B * S), inv_freq.reshape(half, 1), place,
      mix_pre.reshape(1, D_MODEL), w_in.astype(_BF16), w_pool.astype(_BF16), pool_scale.reshape(1, POOL_WIDTH),
      g_pool.reshape(1, POOL_WIDTH), g_attn.reshape(1, ATTN_WIDTH), w_out.astype(_BF16),
      mix_post.reshape(1, D_MODEL))
    return out.reshape(B, S, D_MODEL)


def kernel(x, positions, ffn1_pre, ffn1_w_gu, ffn1_w_down, ffn1_post, mix_pre, w_in, w_pool, pool_scale, sinks, g_pool, g_attn, w_out, mix_post, ffn2_pre, ffn2_w_gu, ffn2_w_down, ffn2_post):
    B, S, D = x.shape
    for l in range(ffn1_pre.shape[0]):
        x = _ffn(x.reshape(B * S, D), ffn1_pre[l], ffn1_w_gu[l], ffn1_w_down[l], ffn1_post[l]).reshape(B, S, D)
        x = _mixer(x, positions, mix_pre[l], w_in[l], w_pool[l], pool_scale[l], sinks[l], g_pool[l],
                   g_attn[l], w_out[l], mix_post[l])
        x = _ffn(x.reshape(B * S, D), ffn2_pre[l], ffn2_w_gu[l], ffn2_w_down[l], ffn2_post[l]).reshape(B, S, D)
    return x
```

```python
import collections
import functools

import jax
import jax.numpy as jnp
from jax import lax
from jax.experimental import pallas as pl
from jax.experimental.pallas import tpu as pltpu

D_MODEL = 1024
D_FF = 2816
POOL_WINDOWS = (2, 4, 8, 16)
POOL_WIDTH = 512
POOL_GROUP = 128
HEAD_DIM = 64
N_HEADS = 8
N_KV_HEADS = 2
ATTN_WIDTH = 512
KV_WIDTH = 128
BLOCK = 128
ROPE_THETA = 500000.0
ROT_DIM = 16
MIX_WIDTH = 1024
IN_WIDTH = 1280
EPS = 1e-6
NEG_INF = -1e30
LOG2E = 1.4426950408889634

LANES = 128
BF16_SUBLANES = 16
POOL_HALO = 16

FFN_TOKENS = 512
FFN_SUBTILES = 1
FFN_CHUNK = 256
MIX_TOKENS = 512
FFN_VMEM_BYTES = 56 * 1024 * 1024
MIX_VMEM_BYTES = 56 * 1024 * 1024

_F32 = jnp.float32
_BF16 = jnp.bfloat16


def _rms(x, g):
    return x * lax.rsqrt(jnp.mean(x * x, axis=-1, keepdims=True) + EPS) * g


def _ffn_kernel(n_cast, x_ref, pre_ref, wgu_ref, wd_ref, post_ref, *rest):
    cast_in, o_ref, cast_out = rest[:n_cast], rest[n_cast], rest[n_cast + 1:2 * n_cast + 1]
    xn_ref, act_ref = rest[2 * n_cast + 1:]
    for src, dst in zip(cast_in, cast_out):
        dst[...] = src[...].astype(_BF16)
    T = FFN_TOKENS
    for t in range(FFN_SUBTILES):
        x = x_ref[t * T:(t + 1) * T, :]
        xn_ref[...] = _rms(x, pre_ref[...]).astype(_BF16)
        for c in range(D_FF // FFN_CHUNK):
            lo = c * FFN_CHUNK
            xn = xn_ref[...]
            g = jnp.dot(xn, wgu_ref[:, lo:lo + FFN_CHUNK], preferred_element_type=_F32)
            u = jnp.dot(xn, wgu_ref[:, D_FF + lo:D_FF + lo + FFN_CHUNK], preferred_element_type=_F32)
            act_ref[:, lo:lo + FFN_CHUNK] = (g * jax.nn.sigmoid(g) * u).astype(_BF16)
        h = jnp.dot(act_ref[...], wd_ref[...], preferred_element_type=_F32)
        o_ref[t * T:(t + 1) * T, :] = x + _rms(h, post_ref[...])


def _ffn(x2d, pre, w_gu, w_down, post, later_weights=()):
    n_tok = x2d.shape[0]
    step_tokens = FFN_SUBTILES * FFN_TOKENS
    n_steps = n_tok // step_tokens
    const = lambda i: (0, 0)
    resident = dict(pipeline_mode=pl.Buffered(1))
    slabs = [w.reshape(n_steps, BF16_SUBLANES, w.size // (n_steps * BF16_SUBLANES)) for w in later_weights]
    slab_specs = [pl.BlockSpec((1,) + s.shape[1:], lambda i: (i, 0, 0)) for s in slabs]
    outs = pl.pallas_call(
        functools.partial(_ffn_kernel, len(slabs)),
        name="ffn",
        grid=(n_steps,),
        in_specs=[
            pl.BlockSpec((step_tokens, D_MODEL), lambda i: (i, 0)),
            pl.BlockSpec((1, D_MODEL), const, **resident),
            pl.BlockSpec((D_MODEL, 2 * D_FF), const, **resident),
            pl.BlockSpec((D_FF, D_MODEL), const, **resident),
            pl.BlockSpec((1, D_MODEL), const, **resident),
        ] + slab_specs,
        out_specs=[pl.BlockSpec((step_tokens, D_MODEL), lambda i: (i, 0))] + slab_specs,
        out_shape=[jax.ShapeDtypeStruct(x2d.shape, _F32)]
        + [jax.ShapeDtypeStruct(s.shape, _BF16) for s in slabs],
        scratch_shapes=[pltpu.VMEM((FFN_TOKENS, D_MODEL), _BF16), pltpu.VMEM((FFN_TOKENS, D_FF), _BF16)],
        compiler_params=pltpu.CompilerParams(
            dimension_semantics=("arbitrary",), vmem_limit_bytes=FFN_VMEM_BYTES),
    )(x2d, pre.reshape(1, D_MODEL), w_gu, w_down, (0.5 * post).reshape(1, D_MODEL), *slabs)
    return outs[0], [o.reshape(w.shape) for o, w in zip(outs[1:], later_weights)]


_TileScratch = collections.namedtuple("_TileScratch", "ubuf kbuf vbuf qbuf attn sbuf pbuf ibuf")
_Params = collections.namedtuple(
    "_Params", "sinks invf place pre win wpool pscale gpool gattn wout post band hn")


def _tile_scratch_shapes(T):
    return [
        pltpu.VMEM((T + POOL_HALO, POOL_WIDTH), _F32),
        pltpu.VMEM((4, T + BLOCK, LANES), _BF16),
        pltpu.VMEM((4, T + BLOCK, LANES), _BF16),
        pltpu.VMEM((ATTN_WIDTH // LANES, T, LANES), _BF16),
        pltpu.VMEM((ATTN_WIDTH // LANES, T, LANES), _F32),
        pltpu.VMEM((2, 2, 2 * BLOCK, 2 * BLOCK), _F32),
        pltpu.VMEM((2, 2, 2 * BLOCK, 2 * BLOCK), _BF16),
        pltpu.VMEM((2, 2, 2 * BLOCK, LANES), _F32),
    ]


def _rope(t, cos_t, sin_t, lo_lanes):
    partner = jnp.where(lo_lanes, pltpu.roll(t, LANES - ROT_DIM // 2, 1), pltpu.roll(t, ROT_DIM // 2, 1))
    return t * cos_t + partner * sin_t


def _project(x, pos_row, prm, cur, prev, is_start):
    T = x.shape[0]
    prm.hn[...] = _rms(x, prm.pre[...]).astype(_BF16)

    def win_cols(lo, n):
        return jnp.dot(prm.hn[...], prm.win[:, lo:lo + n], preferred_element_type=_F32)

    if prev is None:
        cur.ubuf[0:POOL_HALO, :] = jnp.zeros((POOL_HALO, POOL_WIDTH), _F32)
        cur.kbuf[:, 0:BLOCK, :] = jnp.zeros((4, BLOCK, LANES), _BF16)
        cur.vbuf[:, 0:BLOCK, :] = jnp.zeros((4, BLOCK, LANES), _BF16)
    else:
        cur.ubuf[0:POOL_HALO, :] = jnp.where(is_start, 0.0, prev.ubuf[T:T + POOL_HALO, :])
        cur.kbuf[:, 0:BLOCK, :] = jnp.where(is_start, 0.0, prev.kbuf[:, T:T + BLOCK, :]).astype(_BF16)
        cur.vbuf[:, 0:BLOCK, :] = jnp.where(is_start, 0.0, prev.vbuf[:, T:T + BLOCK, :]).astype(_BF16)
    for lo in range(0, POOL_WIDTH, 2 * LANES):
        cur.ubuf[POOL_HALO:POOL_HALO + T, lo:lo + 2 * LANES] = win_cols(lo, 2 * LANES)

    lane = lax.broadcasted_iota(jnp.int32, (1, LANES), 1)
    dim = lane % HEAD_DIM
    lo_lanes = dim < ROT_DIM // 2
    ang = pos_row.astype(_F32) * prm.invf[...]
    cs = jnp.concatenate([jnp.cos(ang), jnp.sin(ang)], axis=0)
    cs_hi = cs.astype(_BF16)
    rem = cs - cs_hi.astype(_F32)
    cs_mid = rem.astype(_BF16)
    cs_lo = (rem - cs_mid.astype(_F32)).astype(_BF16)
    cs3 = jnp.concatenate([cs_hi, cs_mid, cs_lo], axis=0)
    tables = lax.dot_general(cs3, prm.place[...], (((0,), (0,)), ((), ())), preferred_element_type=_F32)
    cos_t = jnp.where(dim < ROT_DIM, tables[:, :LANES], 1.0)
    sin_t = tables[:, LANES:]

    scale = HEAD_DIM ** -0.5 * LOG2E
    for p in range(0, ATTN_WIDTH // LANES, 2):
        q2 = win_cols(POOL_WIDTH + p * LANES, 2 * LANES)
        for j in range(2):
            qp = q2[:, j * LANES:(j + 1) * LANES]
            cur.qbuf[p + j] = (_rope(qp, cos_t, sin_t, lo_lanes) * scale).astype(_BF16)

    first_half = lane < HEAD_DIM
    kv_cols = win_cols(MIX_WIDTH, 2 * KV_WIDTH)
    k = _rope(kv_cols[:, :KV_WIDTH], cos_t, sin_t, lo_lanes)
    v = kv_cols[:, KV_WIDTH:]
    for buf, t in ((cur.kbuf, k), (cur.vbuf, v)):
        t_sw = pltpu.roll(t, HEAD_DIM, 1)
        variants = (jnp.where(first_half, t, 0.0), jnp.where(first_half, 0.0, t_sw),
                    jnp.where(first_half, t_sw, 0.0), jnp.where(first_half, 0.0, t))
        for idx, var in enumerate(variants):
            buf[idx, BLOCK:BLOCK + T, :] = var.astype(_BF16)


def _mix(x_ref, row0, prm, cur, s_tile):
    T = MIX_TOKENS
    is_start = s_tile == 0

    row = lax.broadcasted_iota(jnp.int32, (T, 1), 0)
    seen = s_tile * T + row + 1
    pooled = []
    for g, w in enumerate(POOL_WINDOWS):
        lo = g * POOL_GROUP
        win_sum = cur.ubuf[:, lo:lo + POOL_GROUP]
        span = 1
        while span < w:
            win_sum = win_sum + pltpu.roll(win_sum, span, 0)
            span *= 2
        cnt = jnp.minimum(seen, w).astype(_F32)
        d = win_sum[POOL_HALO:, :] / cnt - cur.ubuf[POOL_HALO:POOL_HALO + T, lo:lo + POOL_GROUP]
        pooled.append(jnp.dot(d.astype(_BF16), prm.wpool[g], preferred_element_type=_F32))
    pool_out = jnp.concatenate(pooled, axis=1) * prm.pscale[...]

    upper_rows = lax.broadcasted_iota(jnp.int32, (2 * BLOCK, 1), 0) < BLOCK
    nt = (((1,), (1,)), ((), ()))
    n_units = N_KV_HEADS * T // BLOCK
    ones = jnp.ones((2 * BLOCK, LANES), _BF16)

    def rows(u, n):
        return pl.ds((u // N_KV_HEADS) * BLOCK, n)

    def scores(u):
        kv, slot = u % N_KV_HEADS, u % 2
        q2 = jnp.concatenate([cur.qbuf[2 * kv, rows(u, BLOCK), :], cur.qbuf[2 * kv + 1, rows(u, BLOCK), :]],
                             axis=0)
        valid = prm.band[jnp.where(is_start, 1, 0) if u < N_KV_HEADS else 0] > 0.5
        for e in range(2):
            s = lax.dot_general(q2, cur.kbuf[2 * kv + e, rows(u, 2 * BLOCK), :], nt, preferred_element_type=_F32)
            cur.sbuf[slot, e] = jnp.where(valid, s, NEG_INF)

    def softmax(u):
        kv, slot = u % N_KV_HEADS, u % 2
        for e in range(2):
            head_a = 4 * kv + e
            sink = jnp.where(upper_rows, prm.sinks[head_a] * LOG2E, prm.sinks[head_a + 2] * LOG2E)
            s = cur.sbuf[slot, e]
            m = jnp.maximum(jnp.max(s, axis=1, keepdims=True), sink)
            cur.pbuf[slot, e] = jnp.exp2(s - m).astype(_BF16)
            cur.ibuf[slot, e] = jnp.broadcast_to(jnp.exp2(sink - m), (2 * BLOCK, LANES))

    def weighted(u):
        kv, slot = u % N_KV_HEADS, u % 2
        out2 = None
        for e in range(2):
            v_ones = jnp.concatenate([cur.vbuf[2 * kv + e, rows(u, 2 * BLOCK), :], ones], axis=1)
            pv = jnp.dot(cur.pbuf[slot, e], v_ones, preferred_element_type=_F32)
            pv = pv[:, :LANES] / (pv[:, LANES:] + cur.ibuf[slot, e])
            out2 = pv if out2 is None else out2 + pv
        cur.attn[2 * kv, rows(u, BLOCK), :] = out2[:BLOCK]
        cur.attn[2 * kv + 1, rows(u, BLOCK), :] = out2[BLOCK:]

    scores(0)
    scores(1)
    softmax(0)
    for u in range(2, n_units):
        weighted(u - 2)
        softmax(u - 1)
        scores(u)
    softmax(n_units - 1)
    weighted(n_units - 2)
    weighted(n_units - 1)

    y_pool = _rms(pool_out, prm.gpool[...]).astype(_BF16)
    attn = jnp.concatenate([cur.attn[p] for p in range(ATTN_WIDTH // LANES)], axis=1)
    y_attn = _rms(attn, prm.gattn[...]).astype(_BF16)
    hout = (jnp.dot(y_pool, prm.wout[0:POOL_WIDTH, :], preferred_element_type=_F32)
            + jnp.dot(y_attn, prm.wout[POOL_WIDTH:, :], preferred_element_type=_F32))
    return x_ref[row0:row0 + T, :] + _rms(hout, prm.post[...])


def _mixer_kernel(tiles_per_seq, sinks_ref, x_ref, xn_ref, pos_ref, posn_ref, invf_ref, place_ref, pre_ref,
                  win_ref, wpool_ref, pscale_ref, gpool_ref, gattn_ref, wout_ref, post_ref, o_ref,
                  band_ref, hn_ref, *scratch):
    T = MIX_TOKENS
    n_set = len(scratch) // 2
    set0, set1 = _TileScratch(*scratch[:n_set]), _TileScratch(*scratch[n_set:])
    prm = _Params(sinks_ref, invf_ref, place_ref, pre_ref, win_ref, wpool_ref, pscale_ref, gpool_ref,
                  gattn_ref, wout_ref, post_ref, band_ref, hn_ref)
    g = pl.program_id(0)
    s_even = (2 * g) % tiles_per_seq

    @pl.when(g == 0)
    def _():
        qi = lax.broadcasted_iota(jnp.int32, (2 * BLOCK, 2 * BLOCK), 0) % BLOCK
        kj = lax.broadcasted_iota(jnp.int32, (2 * BLOCK, 2 * BLOCK), 1)
        band = (kj > qi) & (kj <= qi + BLOCK)
        band_ref[0] = band.astype(_F32)
        band_ref[1] = (band & (kj >= BLOCK)).astype(_F32)
        _project(x_ref[0:T, :], pos_ref[:, 0:T], prm, set0, None, True)

    _project(x_ref[T:2 * T, :], pos_ref[:, T:2 * T], prm, set1, set0, False)
    o_ref[0:T, :] = _mix(x_ref, 0, prm, set0, s_even)

    next_start = (2 * g + 2) % tiles_per_seq == 0
    _project(xn_ref[...], posn_ref[...], prm, set0, set1, next_start)
    o_ref[T:2 * T, :] = _mix(x_ref, T, prm, set1, s_even + 1)


def _mixer(x, positions, mix_pre, w_in, w_pool, pool_scale, sinks, g_pool, g_attn, w_out, mix_post):
    B, S, _ = x.shape
    T = MIX_TOKENS
    n_tiles = B * S // T
    assert S % (2 * T) == 0
    half = ROT_DIM // 2
    inv_freq = ROPE_THETA ** (-jnp.arange(0, ROT_DIM, 2, dtype=_F32) / ROT_DIM)
    dim = jnp.arange(LANES) % HEAD_DIM
    freq = jnp.arange(half)[:, None]
    cos_place = ((dim[None, :] < ROT_DIM) & (dim[None, :] % half == freq)).astype(_F32)
    sin_place = cos_place * jnp.where(dim < half, -1.0, 1.0)[None, :]
    zeros = jnp.zeros_like(cos_place)
    place = jnp.concatenate([jnp.concatenate([cos_place, zeros], axis=1),
                             jnp.concatenate([zeros, sin_place], axis=1)], axis=0)
    place = jnp.tile(place, (3, 1)).astype(_BF16)

    resident = dict(pipeline_mode=pl.Buffered(1))
    full = lambda shape: pl.BlockSpec(shape, lambda g, sinks_ref: (0,) * len(shape), **resident)
    next_tile = lambda g, sinks_ref: jnp.minimum(2 * g + 2, n_tiles - 1)
    grid_spec = pltpu.PrefetchScalarGridSpec(
        num_scalar_prefetch=1,
        grid=(n_tiles // 2,),
        in_specs=[
            pl.BlockSpec((2 * T, D_MODEL), lambda g, sinks_ref: (g, 0)),
            pl.BlockSpec((T, D_MODEL), lambda g, sinks_ref: (next_tile(g, sinks_ref), 0)),
            pl.BlockSpec((1, 2 * T), lambda g, sinks_ref: (0, g)),
            pl.BlockSpec((1, T), lambda g, sinks_ref: (0, next_tile(g, sinks_ref))),
            full((half, 1)),
            full((3 * ROT_DIM, 2 * LANES)),
            full((1, D_MODEL)),
            full((D_MODEL, IN_WIDTH)),
            full((len(POOL_WINDOWS), POOL_GROUP, POOL_GROUP)),
            full((1, POOL_WIDTH)),
            full((1, POOL_WIDTH)),
            full((1, ATTN_WIDTH)),
            full((MIX_WIDTH, D_MODEL)),
            full((1, D_MODEL)),
        ],
        out_specs=pl.BlockSpec((2 * T, D_MODEL), lambda g, sinks_ref: (g, 0)),
        scratch_shapes=[pltpu.VMEM((2, 2 * BLOCK, 2 * BLOCK), _F32), pltpu.VMEM((T, D_MODEL), _BF16)]
        + _tile_scratch_shapes(T) + _tile_scratch_shapes(T),
    )
    out = pl.pallas_call(
        functools.partial(_mixer_kernel, S // T),
        name="mixer",
        grid_spec=grid_spec,
        out_shape=jax.ShapeDtypeStruct((B * S, D_MODEL), _F32),
        compiler_params=pltpu.CompilerParams(
            dimension_semantics=("arbitrary",), vmem_limit_bytes=MIX_VMEM_BYTES),
    )(sinks.astype(_F32), x.reshape(B * S, D_MODEL), x.reshape(B * S, D_MODEL),
      positions.reshape(1, B * S), positions.reshape(1, B * S), inv_freq.reshape(half, 1), place,
      mix_pre.reshape(1, D_MODEL), w_in, w_pool.astype(_BF16), pool_scale.reshape(1, POOL_WIDTH),
      g_pool.reshape(1, POOL_WIDTH), g_attn.reshape(1, ATTN_WIDTH), w_out, mix_post.reshape(1, D_MODEL))
    return out.reshape(B, S, D_MODEL)


def kernel(x, positions, ffn1_pre, ffn1_w_gu, ffn1_w_down, ffn1_post, mix_pre, w_in, w_pool, pool_scale, sinks, g_pool, g_attn, w_out, mix_post, ffn2_pre, ffn2_w_gu, ffn2_w_down, ffn2_post):
    B, S, D = x.shape
    for l in range(ffn1_pre.shape[0]):
        x, (w_in_b, w_out_b, w_gu2_b, w_down2_b) = _ffn(
            x.reshape(B * S, D), ffn1_pre[l], ffn1_w_gu[l].astype(_BF16), ffn1_w_down[l].astype(_BF16),
            ffn1_post[l], later_weights=(w_in[l], w_out[l], ffn2_w_gu[l], ffn2_w_down[l]))
        x = _mixer(x.reshape(B, S, D), positions, mix_pre[l], w_in_b, w_pool[l], pool_scale[l], sinks[l],
                   g_pool[l], g_attn[l], w_out_b, mix_post[l])
        x, _ = _ffn(x.reshape(B * S, D), ffn2_pre[l], w_gu2_b, w_down2_b, ffn2_post[l])
        x = x.reshape(B, S, D)
    return x
```

```python
import collections
import functools

import jax
import jax.numpy as jnp
from jax import lax
from jax.experimental import pallas as pl
from jax.experimental.pallas import tpu as pltpu

D_MODEL = 1024
D_FF = 2816
POOL_WINDOWS = (2, 4, 8, 16)
POOL_WIDTH = 512
POOL_GROUP = 128
HEAD_DIM = 64
N_HEADS = 8
N_KV_HEADS = 2
ATTN_WIDTH = 512
KV_WIDTH = 128
BLOCK = 128
ROPE_THETA = 500000.0
ROT_DIM = 16
MIX_WIDTH = 1024
IN_WIDTH = 1280
EPS = 1e-6
NEG_INF = -1e30
LOG2E = 1.4426950408889634

LANES = 128
BF16_SUBLANES = 16
POOL_HALO = 16

FFN_TOKENS = 512
FFN_SUBTILES = 1
FFN_CHUNK = 256
MIX_TOKENS = 512
FFN_VMEM_BYTES = 56 * 1024 * 1024
MIX_VMEM_BYTES = 56 * 1024 * 1024

_F32 = jnp.float32
_BF16 = jnp.bfloat16


def _rms(x, g):
    return x * lax.rsqrt(jnp.mean(x * x, axis=-1, keepdims=True) + EPS) * g


def _ffn_kernel(n_cast, x_ref, pre_ref, wgu_ref, wd_ref, post_ref, *rest):
    cast_in, o_ref, cast_out = rest[:n_cast], rest[n_cast], rest[n_cast + 1:2 * n_cast + 1]
    xn_ref, act_ref = rest[2 * n_cast + 1:]
    for src, dst in zip(cast_in, cast_out):
        dst[...] = src[...].astype(_BF16)
    T = FFN_TOKENS
    for t in range(FFN_SUBTILES):
        x = x_ref[t * T:(t + 1) * T, :]
        xn_ref[...] = _rms(x, pre_ref[...]).astype(_BF16)
        for c in range(D_FF // FFN_CHUNK):
            lo = c * FFN_CHUNK
            xn = xn_ref[...]
            g = jnp.dot(xn, wgu_ref[:, lo:lo + FFN_CHUNK], preferred_element_type=_F32)
            u = jnp.dot(xn, wgu_ref[:, D_FF + lo:D_FF + lo + FFN_CHUNK], preferred_element_type=_F32)
            act_ref[:, lo:lo + FFN_CHUNK] = (g * jax.nn.sigmoid(g) * u).astype(_BF16)
        h = jnp.dot(act_ref[...], wd_ref[...], preferred_element_type=_F32)
        o_ref[t * T:(t + 1) * T, :] = x + _rms(h, post_ref[...])


def _ffn(x2d, pre, w_gu, w_down, post, later_weights=()):
    n_tok = x2d.shape[0]
    step_tokens = FFN_SUBTILES * FFN_TOKENS
    n_steps = n_tok // step_tokens
    const = lambda i: (0, 0)
    resident = dict(pipeline_mode=pl.Buffered(1))
    slab_in, slab_out, slab_shapes = [], [], []
    for w, layer in later_weights:
        _, n_rows, n_cols = w.shape
        slab = next(r for r in range(BF16_SUBLANES, n_rows + 1, BF16_SUBLANES)
                    if n_rows % r == 0 and n_rows // r <= n_steps)
        last = n_rows // slab - 1
        slab_in.append(pl.BlockSpec((None, slab, n_cols), lambda i, layer=layer, last=last:
                                    (layer, jnp.minimum(i, last), 0)))
        slab_out.append(pl.BlockSpec((slab, n_cols), lambda i, last=last: (jnp.minimum(i, last), 0)))
        slab_shapes.append(jax.ShapeDtypeStruct((n_rows, n_cols), _BF16))
    outs = pl.pallas_call(
        functools.partial(_ffn_kernel, len(later_weights)),
        name="ffn",
        grid=(n_steps,),
        in_specs=[
            pl.BlockSpec((step_tokens, D_MODEL), lambda i: (i, 0)),
            pl.BlockSpec((1, D_MODEL), const, **resident),
            pl.BlockSpec((D_MODEL, 2 * D_FF), const, **resident),
            pl.BlockSpec((D_FF, D_MODEL), const, **resident),
            pl.BlockSpec((1, D_MODEL), const, **resident),
        ] + slab_in,
        out_specs=[pl.BlockSpec((step_tokens, D_MODEL), lambda i: (i, 0))] + slab_out,
        out_shape=[jax.ShapeDtypeStruct(x2d.shape, _F32)] + slab_shapes,
        scratch_shapes=[pltpu.VMEM((FFN_TOKENS, D_MODEL), _BF16), pltpu.VMEM((FFN_TOKENS, D_FF), _BF16)],
        compiler_params=pltpu.CompilerParams(
            dimension_semantics=("arbitrary",), vmem_limit_bytes=FFN_VMEM_BYTES),
    )(x2d, pre.reshape(1, D_MODEL), w_gu, w_down, (0.5 * post).reshape(1, D_MODEL),
      *[w for w, _ in later_weights])
    return outs[0], list(outs[1:])


_TileScratch = collections.namedtuple("_TileScratch", "ubuf kbuf vbuf qbuf attn sbuf pbuf ibuf")
_Params = collections.namedtuple(
    "_Params", "sinks invf place pre win wpool pscale gpool gattn wout post band hn")


def _tile_scratch_shapes(T):
    return [
        pltpu.VMEM((T + POOL_HALO, POOL_WIDTH), _F32),
        pltpu.VMEM((4, T + BLOCK, LANES), _BF16),
        pltpu.VMEM((4, T + BLOCK, LANES), _BF16),
        pltpu.VMEM((ATTN_WIDTH // LANES, T, LANES), _BF16),
        pltpu.VMEM((ATTN_WIDTH // LANES, T, LANES), _F32),
        pltpu.VMEM((2, 2, 2 * BLOCK, 2 * BLOCK), _F32),
        pltpu.VMEM((2, 2, 2 * BLOCK, 2 * BLOCK), _BF16),
        pltpu.VMEM((2, 2, 2 * BLOCK, LANES), _F32),
    ]


def _rope(t, cos_t, sin_t, lo_lanes):
    partner = jnp.where(lo_lanes, pltpu.roll(t, LANES - ROT_DIM // 2, 1), pltpu.roll(t, ROT_DIM // 2, 1))
    return t * cos_t + partner * sin_t


def _project(x, pos_row, prm, cur, prev, is_start):
    T = x.shape[0]
    prm.hn[...] = _rms(x, prm.pre[...]).astype(_BF16)

    def win_cols(lo, n):
        return jnp.dot(prm.hn[...], prm.win[:, lo:lo + n], preferred_element_type=_F32)

    if prev is None:
        cur.ubuf[0:POOL_HALO, :] = jnp.zeros((POOL_HALO, POOL_WIDTH), _F32)
        cur.kbuf[:, 0:BLOCK, :] = jnp.zeros((4, BLOCK, LANES), _BF16)
        cur.vbuf[:, 0:BLOCK, :] = jnp.zeros((4, BLOCK, LANES), _BF16)
    else:
        cur.ubuf[0:POOL_HALO, :] = jnp.where(is_start, 0.0, prev.ubuf[T:T + POOL_HALO, :])
        cur.kbuf[:, 0:BLOCK, :] = jnp.where(is_start, 0.0, prev.kbuf[:, T:T + BLOCK, :]).astype(_BF16)
        cur.vbuf[:, 0:BLOCK, :] = jnp.where(is_start, 0.0, prev.vbuf[:, T:T + BLOCK, :]).astype(_BF16)
    for lo in range(0, POOL_WIDTH, 2 * LANES):
        cur.ubuf[POOL_HALO:POOL_HALO + T, lo:lo + 2 * LANES] = win_cols(lo, 2 * LANES)

    lane = lax.broadcasted_iota(jnp.int32, (1, LANES), 1)
    dim = lane % HEAD_DIM
    lo_lanes = dim < ROT_DIM // 2
    ang = pos_row.astype(_F32) * prm.invf[...]
    cs = jnp.concatenate([jnp.cos(ang), jnp.sin(ang)], axis=0)
    cs_hi = cs.astype(_BF16)
    rem = cs - cs_hi.astype(_F32)
    cs_mid = rem.astype(_BF16)
    cs_lo = (rem - cs_mid.astype(_F32)).astype(_BF16)
    cs3 = jnp.concatenate([cs_hi, cs_mid, cs_lo], axis=0)
    tables = lax.dot_general(cs3, prm.place[...], (((0,), (0,)), ((), ())), preferred_element_type=_F32)
    cos_t = jnp.where(dim < ROT_DIM, tables[:, :LANES], 1.0)
    sin_t = tables[:, LANES:]

    scale = HEAD_DIM ** -0.5 * LOG2E
    for p in range(0, ATTN_WIDTH // LANES, 2):
        q2 = win_cols(POOL_WIDTH + p * LANES, 2 * LANES)
        for j in range(2):
            qp = q2[:, j * LANES:(j + 1) * LANES]
            cur.qbuf[p + j] = (_rope(qp, cos_t, sin_t, lo_lanes) * scale).astype(_BF16)

    first_half = lane < HEAD_DIM
    kv_cols = win_cols(MIX_WIDTH, 2 * KV_WIDTH)
    k = _rope(kv_cols[:, :KV_WIDTH], cos_t, sin_t, lo_lanes)
    v = kv_cols[:, KV_WIDTH:]
    for buf, t in ((cur.kbuf, k), (cur.vbuf, v)):
        t_sw = pltpu.roll(t, HEAD_DIM, 1)
        variants = (jnp.where(first_half, t, 0.0), jnp.where(first_half, 0.0, t_sw),
                    jnp.where(first_half, t_sw, 0.0), jnp.where(first_half, 0.0, t))
        for idx, var in enumerate(variants):
            buf[idx, BLOCK:BLOCK + T, :] = var.astype(_BF16)


def _mix(x_ref, row0, prm, cur, s_tile):
    T = MIX_TOKENS
    is_start = s_tile == 0

    row = lax.broadcasted_iota(jnp.int32, (T, 1), 0)
    seen = s_tile * T + row + 1
    pooled = []
    for g, w in enumerate(POOL_WINDOWS):
        lo = g * POOL_GROUP
        win_sum = cur.ubuf[:, lo:lo + POOL_GROUP]
        span = 1
        while span < w:
            win_sum = win_sum + pltpu.roll(win_sum, span, 0)
            span *= 2
        cnt = jnp.minimum(seen, w).astype(_F32)
        d = win_sum[POOL_HALO:, :] / cnt - cur.ubuf[POOL_HALO:POOL_HALO + T, lo:lo + POOL_GROUP]
        pooled.append(jnp.dot(d.astype(_BF16), prm.wpool[g], preferred_element_type=_F32))
    pool_out = jnp.concatenate(pooled, axis=1) * prm.pscale[...]

    upper_rows = lax.broadcasted_iota(jnp.int32, (2 * BLOCK, 1), 0) < BLOCK
    nt = (((1,), (1,)), ((), ()))
    n_units = N_KV_HEADS * T // BLOCK
    ones = jnp.ones((2 * BLOCK, LANES), _BF16)

    def rows(u, n):
        return pl.ds((u // N_KV_HEADS) * BLOCK, n)

    def scores(u):
        kv, slot = u % N_KV_HEADS, u % 2
        q2 = jnp.concatenate([cur.qbuf[2 * kv, rows(u, BLOCK), :], cur.qbuf[2 * kv + 1, rows(u, BLOCK), :]],
                             axis=0)
        valid = prm.band[jnp.where(is_start, 1, 0) if u < N_KV_HEADS else 0] > 0.5
        for e in range(2):
            s = lax.dot_general(q2, cur.kbuf[2 * kv + e, rows(u, 2 * BLOCK), :], nt, preferred_element_type=_F32)
            cur.sbuf[slot, e] = jnp.where(valid, s, NEG_INF)

    def softmax(u):
        kv, slot = u % N_KV_HEADS, u % 2
        for e in range(2):
            head_a = 4 * kv + e
            sink = jnp.where(upper_rows, prm.sinks[head_a] * LOG2E, prm.sinks[head_a + 2] * LOG2E)
            s = cur.sbuf[slot, e]
            m = jnp.maximum(jnp.max(s, axis=1, keepdims=True), sink)
            cur.pbuf[slot, e] = jnp.exp2(s - m).astype(_BF16)
            cur.ibuf[slot, e] = jnp.broadcast_to(jnp.exp2(sink - m), (2 * BLOCK, LANES))

    def weighted(u):
        kv, slot = u % N_KV_HEADS, u % 2
        out2 = None
        for e in range(2):
            v_ones = jnp.concatenate([cur.vbuf[2 * kv + e, rows(u, 2 * BLOCK), :], ones], axis=1)
            pv = jnp.dot(cur.pbuf[slot, e], v_ones, preferred_element_type=_F32)
            pv = pv[:, :LANES] / (pv[:, LANES:] + cur.ibuf[slot, e])
            out2 = pv if out2 is None else out2 + pv
        cur.attn[2 * kv, rows(u, BLOCK), :] = out2[:BLOCK]
        cur.attn[2 * kv + 1, rows(u, BLOCK), :] = out2[BLOCK:]

    scores(0)
    scores(1)
    softmax(0)
    for u in range(2, n_units):
        weighted(u - 2)
        softmax(u - 1)
        scores(u)
    softmax(n_units - 1)
    weighted(n_units - 2)
    weighted(n_units - 1)

    y_pool = _rms(pool_out, prm.gpool[...]).astype(_BF16)
    attn = jnp.concatenate([cur.attn[p] for p in range(ATTN_WIDTH // LANES)], axis=1)
    y_attn = _rms(attn, prm.gattn[...]).astype(_BF16)
    hout = (jnp.dot(y_pool, prm.wout[0:POOL_WIDTH, :], preferred_element_type=_F32)
            + jnp.dot(y_attn, prm.wout[POOL_WIDTH:, :], preferred_element_type=_F32))
    return x_ref[row0:row0 + T, :] + _rms(hout, prm.post[...])


def _mixer_kernel(tiles_per_seq, sinks_ref, x_ref, xn_ref, pos_ref, posn_ref, invf_ref, place_ref, pre_ref,
                  win_ref, wpool_ref, pscale_ref, gpool_ref, gattn_ref, wout_ref, post_ref, o_ref,
                  band_ref, hn_ref, *scratch):
    T = MIX_TOKENS
    n_set = len(scratch) // 2
    set0, set1 = _TileScratch(*scratch[:n_set]), _TileScratch(*scratch[n_set:])
    prm = _Params(sinks_ref, invf_ref, place_ref, pre_ref, win_ref, wpool_ref, pscale_ref, gpool_ref,
                  gattn_ref, wout_ref, post_ref, band_ref, hn_ref)
    g = pl.program_id(0)
    s_even = (2 * g) % tiles_per_seq

    @pl.when(g == 0)
    def _():
        qi = lax.broadcasted_iota(jnp.int32, (2 * BLOCK, 2 * BLOCK), 0) % BLOCK
        kj = lax.broadcasted_iota(jnp.int32, (2 * BLOCK, 2 * BLOCK), 1)
        band = (kj > qi) & (kj <= qi + BLOCK)
        band_ref[0] = band.astype(_F32)
        band_ref[1] = (band & (kj >= BLOCK)).astype(_F32)
        _project(x_ref[0:T, :], pos_ref[:, 0:T], prm, set0, None, True)

    _project(x_ref[T:2 * T, :], pos_ref[:, T:2 * T], prm, set1, set0, False)
    o_ref[0:T, :] = _mix(x_ref, 0, prm, set0, s_even)

    next_start = (2 * g + 2) % tiles_per_seq == 0
    _project(xn_ref[...], posn_ref[...], prm, set0, set1, next_start)
    o_ref[T:2 * T, :] = _mix(x_ref, T, prm, set1, s_even + 1)


def _mixer(x, positions, mix_pre, w_in, w_pool, pool_scale, sinks, g_pool, g_attn, w_out, mix_post):
    B, S, _ = x.shape
    T = MIX_TOKENS
    n_tiles = B * S // T
    assert S % (2 * T) == 0
    half = ROT_DIM // 2
    inv_freq = ROPE_THETA ** (-jnp.arange(0, ROT_DIM, 2, dtype=_F32) / ROT_DIM)
    dim = jnp.arange(LANES) % HEAD_DIM
    freq = jnp.arange(half)[:, None]
    cos_place = ((dim[None, :] < ROT_DIM) & (dim[None, :] % half == freq)).astype(_F32)
    sin_place = cos_place * jnp.where(dim < half, -1.0, 1.0)[None, :]
    zeros = jnp.zeros_like(cos_place)
    place = jnp.concatenate([jnp.concatenate([cos_place, zeros], axis=1),
                             jnp.concatenate([zeros, sin_place], axis=1)], axis=0)
    place = jnp.tile(place, (3, 1)).astype(_BF16)

    resident = dict(pipeline_mode=pl.Buffered(1))
    full = lambda shape: pl.BlockSpec(shape, lambda g, sinks_ref: (0,) * len(shape), **resident)
    next_tile = lambda g, sinks_ref: jnp.minimum(2 * g + 2, n_tiles - 1)
    grid_spec = pltpu.PrefetchScalarGridSpec(
        num_scalar_prefetch=1,
        grid=(n_tiles // 2,),
        in_specs=[
            pl.BlockSpec((2 * T, D_MODEL), lambda g, sinks_ref: (g, 0)),
            pl.BlockSpec((T, D_MODEL), lambda g, sinks_ref: (next_tile(g, sinks_ref), 0)),
            pl.BlockSpec((1, 2 * T), lambda g, sinks_ref: (0, g)),
            pl.BlockSpec((1, T), lambda g, sinks_ref: (0, next_tile(g, sinks_ref))),
            full((half, 1)),
            full((3 * ROT_DIM, 2 * LANES)),
            full((1, D_MODEL)),
            full((D_MODEL, IN_WIDTH)),
            full((len(POOL_WINDOWS), POOL_GROUP, POOL_GROUP)),
            full((1, POOL_WIDTH)),
            full((1, POOL_WIDTH)),
            full((1, ATTN_WIDTH)),
            full((MIX_WIDTH, D_MODEL)),
            full((1, D_MODEL)),
        ],
        out_specs=pl.BlockSpec((2 * T, D_MODEL), lambda g, sinks_ref: (g, 0)),
        scratch_shapes=[pltpu.VMEM((2, 2 * BLOCK, 2 * BLOCK), _F32), pltpu.VMEM((T, D_MODEL), _BF16)]
        + _tile_scratch_shapes(T) + _tile_scratch_shapes(T),
    )
    out = pl.pallas_call(
        functools.partial(_mixer_kernel, S // T),
        name="mixer",
        grid_spec=grid_spec,
        out_shape=jax.ShapeDtypeStruct((B * S, D_MODEL), _F32),
        compiler_params=pltpu.CompilerParams(
            dimension_semantics=("arbitrary",), vmem_limit_bytes=MIX_VMEM_BYTES),
    )(sinks.astype(_F32), x.reshape(B * S, D_MODEL), x.reshape(B * S, D_MODEL),
      positions.reshape(1, B * S), positions.reshape(1, B * S), inv_freq.reshape(half, 1), place,
      mix_pre.reshape(1, D_MODEL), w_in, w_pool.astype(_BF16), pool_scale.reshape(1, POOL_WIDTH),
      g_pool.reshape(1, POOL_WIDTH), g_attn.reshape(1, ATTN_WIDTH), w_out, mix_post.reshape(1, D_MODEL))
    return out.reshape(B, S, D_MODEL)


def kernel(x, positions, ffn1_pre, ffn1_w_gu, ffn1_w_down, ffn1_post, mix_pre, w_in, w_pool, pool_scale, sinks, g_pool, g_attn, w_out, mix_post, ffn2_pre, ffn2_w_gu, ffn2_w_down, ffn2_post):
    B, S, D = x.shape
    for l in range(ffn1_pre.shape[0]):
        x, (w_in_b, w_out_b, w_gu2_b, w_down2_b) = _ffn(
            x.reshape(B * S, D), ffn1_pre[l], ffn1_w_gu[l].astype(_BF16), ffn1_w_down[l].astype(_BF16),
            ffn1_post[l], later_weights=((w_in, l), (w_out, l), (ffn2_w_gu, l), (ffn2_w_down, l)))
        x = _mixer(x.reshape(B, S, D), positions, mix_pre[l], w_in_b, w_pool[l], pool_scale[l], sinks[l],
                   g_pool[l], g_attn[l], w_out_b, mix_post[l])
        x, _ = _ffn(x.reshape(B * S, D), ffn2_pre[l], w_gu2_b, w_down2_b, ffn2_post[l])
        x = x.reshape(B, S, D)
    return x
```

```python
import collections
import functools

import jax
import jax.numpy as jnp
from jax import lax
from jax.experimental import pallas as pl
from jax.experimental.pallas import tpu as pltpu

D_MODEL = 1024
D_FF = 2816
POOL_WINDOWS = (2, 4, 8, 16)
POOL_WIDTH = 512
POOL_GROUP = 128
HEAD_DIM = 64
N_HEADS = 8
N_KV_HEADS = 2
ATTN_WIDTH = 512
KV_WIDTH = 128
BLOCK = 128
ROPE_THETA = 500000.0
ROT_DIM = 16
MIX_WIDTH = 1024
IN_WIDTH = 1280
EPS = 1e-6
NEG_INF = -1e30
LOG2E = 1.4426950408889634

LANES = 128
BF16_SUBLANES = 16
POOL_HALO = 16

FFN_TOKENS = 1024
FFN_SUBTILES = 1
FFN_CHUNK = 256
MIX_TOKENS = 512
FFN_VMEM_BYTES = 56 * 1024 * 1024
MIX_VMEM_BYTES = 56 * 1024 * 1024

_F32 = jnp.float32
_BF16 = jnp.bfloat16


def _rms(x, g):
    return x * lax.rsqrt(jnp.mean(x * x, axis=-1, keepdims=True) + EPS) * g


def _ffn_kernel(n_cast, x_ref, pre_ref, wgu_ref, wd_ref, post_ref, *rest):
    cast_in, o_ref, cast_out = rest[:n_cast], rest[n_cast], rest[n_cast + 1:2 * n_cast + 1]
    xn_ref, act_ref = rest[2 * n_cast + 1:]
    for src, dst in zip(cast_in, cast_out):
        dst[...] = src[...].astype(_BF16)
    T = FFN_TOKENS
    for t in range(FFN_SUBTILES):
        x = x_ref[t * T:(t + 1) * T, :]
        xn_ref[...] = _rms(x, pre_ref[...]).astype(_BF16)
        for c in range(D_FF // FFN_CHUNK):
            lo = c * FFN_CHUNK
            xn = xn_ref[...]
            g = jnp.dot(xn, wgu_ref[:, lo:lo + FFN_CHUNK], preferred_element_type=_F32)
            u = jnp.dot(xn, wgu_ref[:, D_FF + lo:D_FF + lo + FFN_CHUNK], preferred_element_type=_F32)
            act_ref[:, lo:lo + FFN_CHUNK] = (g * jax.nn.sigmoid(g) * u).astype(_BF16)
        h = jnp.dot(act_ref[...], wd_ref[...], preferred_element_type=_F32)
        o_ref[t * T:(t + 1) * T, :] = x + _rms(h, post_ref[...])


def _ffn(x2d, pre, w_gu, w_down, post, later_weights=()):
    n_tok = x2d.shape[0]
    step_tokens = FFN_SUBTILES * FFN_TOKENS
    n_steps = n_tok // step_tokens
    const = lambda i: (0, 0)
    resident = dict(pipeline_mode=pl.Buffered(1))
    slab_in, slab_out, slab_shapes = [], [], []
    for w, layer in later_weights:
        _, n_rows, n_cols = w.shape
        slab = next(r for r in range(BF16_SUBLANES, n_rows + 1, BF16_SUBLANES)
                    if n_rows % r == 0 and n_rows // r <= n_steps)
        last = n_rows // slab - 1
        slab_in.append(pl.BlockSpec((None, slab, n_cols), lambda i, layer=layer, last=last:
                                    (layer, jnp.minimum(i, last), 0)))
        slab_out.append(pl.BlockSpec((slab, n_cols), lambda i, last=last: (jnp.minimum(i, last), 0)))
        slab_shapes.append(jax.ShapeDtypeStruct((n_rows, n_cols), _BF16))
    outs = pl.pallas_call(
        functools.partial(_ffn_kernel, len(later_weights)),
        name="ffn",
        grid=(n_steps,),
        in_specs=[
            pl.BlockSpec((step_tokens, D_MODEL), lambda i: (i, 0)),
            pl.BlockSpec((1, D_MODEL), const, **resident),
            pl.BlockSpec((D_MODEL, 2 * D_FF), const, **resident),
            pl.BlockSpec((D_FF, D_MODEL), const, **resident),
            pl.BlockSpec((1, D_MODEL), const, **resident),
        ] + slab_in,
        out_specs=[pl.BlockSpec((step_tokens, D_MODEL), lambda i: (i, 0))] + slab_out,
        out_shape=[jax.ShapeDtypeStruct(x2d.shape, _F32)] + slab_shapes,
        scratch_shapes=[pltpu.VMEM((FFN_TOKENS, D_MODEL), _BF16), pltpu.VMEM((FFN_TOKENS, D_FF), _BF16)],
        compiler_params=pltpu.CompilerParams(
            dimension_semantics=("arbitrary",), vmem_limit_bytes=FFN_VMEM_BYTES),
    )(x2d, pre.reshape(1, D_MODEL), w_gu, w_down, (0.5 * post).reshape(1, D_MODEL),
      *[w for w, _ in later_weights])
    return outs[0], list(outs[1:])


_TileScratch = collections.namedtuple("_TileScratch", "ubuf kbuf vbuf qbuf attn sbuf pbuf ibuf")
_Params = collections.namedtuple(
    "_Params", "sinks invf place pre win wpool pscale gpool gattn wout post band hn")


def _tile_scratch_shapes(T):
    return [
        pltpu.VMEM((T + POOL_HALO, POOL_WIDTH), _F32),
        pltpu.VMEM((4, T + BLOCK, LANES), _BF16),
        pltpu.VMEM((4, T + BLOCK, LANES), _BF16),
        pltpu.VMEM((ATTN_WIDTH // LANES, T, LANES), _BF16),
        pltpu.VMEM((ATTN_WIDTH // LANES, T, LANES), _F32),
        pltpu.VMEM((2, 2, 2 * BLOCK, 2 * BLOCK), _F32),
        pltpu.VMEM((2, 2, 2 * BLOCK, 2 * BLOCK), _BF16),
        pltpu.VMEM((2, 2, 2 * BLOCK, LANES), _F32),
    ]


def _rope(t, cos_t, sin_t, lo_lanes):
    partner = jnp.where(lo_lanes, pltpu.roll(t, LANES - ROT_DIM // 2, 1), pltpu.roll(t, ROT_DIM // 2, 1))
    return t * cos_t + partner * sin_t


def _project(x, pos_row, prm, cur, prev, is_start):
    T = x.shape[0]
    prm.hn[...] = _rms(x, prm.pre[...]).astype(_BF16)

    def win_cols(lo, n):
        return jnp.dot(prm.hn[...], prm.win[:, lo:lo + n], preferred_element_type=_F32)

    if prev is None:
        cur.ubuf[0:POOL_HALO, :] = jnp.zeros((POOL_HALO, POOL_WIDTH), _F32)
        cur.kbuf[:, 0:BLOCK, :] = jnp.zeros((4, BLOCK, LANES), _BF16)
        cur.vbuf[:, 0:BLOCK, :] = jnp.zeros((4, BLOCK, LANES), _BF16)
    else:
        cur.ubuf[0:POOL_HALO, :] = jnp.where(is_start, 0.0, prev.ubuf[T:T + POOL_HALO, :])
        cur.kbuf[:, 0:BLOCK, :] = jnp.where(is_start, 0.0, prev.kbuf[:, T:T + BLOCK, :]).astype(_BF16)
        cur.vbuf[:, 0:BLOCK, :] = jnp.where(is_start, 0.0, prev.vbuf[:, T:T + BLOCK, :]).astype(_BF16)
    for lo in range(0, POOL_WIDTH, 2 * LANES):
        cur.ubuf[POOL_HALO:POOL_HALO + T, lo:lo + 2 * LANES] = win_cols(lo, 2 * LANES)

    lane = lax.broadcasted_iota(jnp.int32, (1, LANES), 1)
    dim = lane % HEAD_DIM
    lo_lanes = dim < ROT_DIM // 2
    ang = pos_row.astype(_F32) * prm.invf[...]
    cs = jnp.concatenate([jnp.cos(ang), jnp.sin(ang)], axis=0)
    cs_hi = cs.astype(_BF16)
    rem = cs - cs_hi.astype(_F32)
    cs_mid = rem.astype(_BF16)
    cs_lo = (rem - cs_mid.astype(_F32)).astype(_BF16)
    cs3 = jnp.concatenate([cs_hi, cs_mid, cs_lo], axis=0)
    tables = lax.dot_general(cs3, prm.place[...], (((0,), (0,)), ((), ())), preferred_element_type=_F32)
    cos_t = jnp.where(dim < ROT_DIM, tables[:, :LANES], 1.0)
    sin_t = tables[:, LANES:]

    scale = HEAD_DIM ** -0.5 * LOG2E
    for p in range(0, ATTN_WIDTH // LANES, 2):
        q2 = win_cols(POOL_WIDTH + p * LANES, 2 * LANES)
        for j in range(2):
            qp = q2[:, j * LANES:(j + 1) * LANES]
            cur.qbuf[p + j] = (_rope(qp, cos_t, sin_t, lo_lanes) * scale).astype(_BF16)

    first_half = lane < HEAD_DIM
    kv_cols = win_cols(MIX_WIDTH, 2 * KV_WIDTH)
    k = _rope(kv_cols[:, :KV_WIDTH], cos_t, sin_t, lo_lanes)
    v = kv_cols[:, KV_WIDTH:]
    for buf, t in ((cur.kbuf, k), (cur.vbuf, v)):
        t_sw = pltpu.roll(t, HEAD_DIM, 1)
        variants = (jnp.where(first_half, t, 0.0), jnp.where(first_half, 0.0, t_sw),
                    jnp.where(first_half, t_sw, 0.0), jnp.where(first_half, 0.0, t))
        for idx, var in enumerate(variants):
            buf[idx, BLOCK:BLOCK + T, :] = var.astype(_BF16)


def _mix(x_ref, row0, prm, cur, s_tile):
    T = MIX_TOKENS
    is_start = s_tile == 0

    row = lax.broadcasted_iota(jnp.int32, (T, 1), 0)
    seen = s_tile * T + row + 1
    pooled = []
    for g, w in enumerate(POOL_WINDOWS):
        lo = g * POOL_GROUP
        win_sum = cur.ubuf[:, lo:lo + POOL_GROUP]
        span = 1
        while span < w:
            win_sum = win_sum + pltpu.roll(win_sum, span, 0)
            span *= 2
        cnt = jnp.minimum(seen, w).astype(_F32)
        d = win_sum[POOL_HALO:, :] / cnt - cur.ubuf[POOL_HALO:POOL_HALO + T, lo:lo + POOL_GROUP]
        pooled.append(jnp.dot(d.astype(_BF16), prm.wpool[g], preferred_element_type=_F32))
    pool_out = jnp.concatenate(pooled, axis=1) * prm.pscale[...]

    upper_rows = lax.broadcasted_iota(jnp.int32, (2 * BLOCK, 1), 0) < BLOCK
    nt = (((1,), (1,)), ((), ()))
    n_units = N_KV_HEADS * T // BLOCK
    ones = jnp.ones((2 * BLOCK, LANES), _BF16)

    def rows(u, n):
        return pl.ds((u // N_KV_HEADS) * BLOCK, n)

    def scores(u):
        kv, slot = u % N_KV_HEADS, u % 2
        q2 = jnp.concatenate([cur.qbuf[2 * kv, rows(u, BLOCK), :], cur.qbuf[2 * kv + 1, rows(u, BLOCK), :]],
                             axis=0)
        valid = prm.band[jnp.where(is_start, 1, 0) if u < N_KV_HEADS else 0] > 0.5
        for e in range(2):
            s = lax.dot_general(q2, cur.kbuf[2 * kv + e, rows(u, 2 * BLOCK), :], nt, preferred_element_type=_F32)
            cur.sbuf[slot, e] = jnp.where(valid, s, NEG_INF)

    def softmax(u):
        kv, slot = u % N_KV_HEADS, u % 2
        for e in range(2):
            head_a = 4 * kv + e
            sink = jnp.where(upper_rows, prm.sinks[head_a] * LOG2E, prm.sinks[head_a + 2] * LOG2E)
            s = cur.sbuf[slot, e]
            m = jnp.maximum(jnp.max(s, axis=1, keepdims=True), sink)
            cur.pbuf[slot, e] = jnp.exp2(s - m).astype(_BF16)
            cur.ibuf[slot, e] = jnp.broadcast_to(jnp.exp2(sink - m), (2 * BLOCK, LANES))

    def weighted(u):
        kv, slot = u % N_KV_HEADS, u % 2
        out2 = None
        for e in range(2):
            v_ones = jnp.concatenate([cur.vbuf[2 * kv + e, rows(u, 2 * BLOCK), :], ones], axis=1)
            pv = jnp.dot(cur.pbuf[slot, e], v_ones, preferred_element_type=_F32)
            pv = pv[:, :LANES] / (pv[:, LANES:] + cur.ibuf[slot, e])
            out2 = pv if out2 is None else out2 + pv
        cur.attn[2 * kv, rows(u, BLOCK), :] = out2[:BLOCK]
        cur.attn[2 * kv + 1, rows(u, BLOCK), :] = out2[BLOCK:]

    scores(0)
    scores(1)
    softmax(0)
    for u in range(2, n_units):
        weighted(u - 2)
        softmax(u - 1)
        scores(u)
    softmax(n_units - 1)
    weighted(n_units - 2)
    weighted(n_units - 1)

    y_pool = _rms(pool_out, prm.gpool[...]).astype(_BF16)
    attn = jnp.concatenate([cur.attn[p] for p in range(ATTN_WIDTH // LANES)], axis=1)
    y_attn = _rms(attn, prm.gattn[...]).astype(_BF16)
    hout = (jnp.dot(y_pool, prm.wout[0:POOL_WIDTH, :], preferred_element_type=_F32)
            + jnp.dot(y_attn, prm.wout[POOL_WIDTH:, :], preferred_element_type=_F32))
    return x_ref[row0:row0 + T, :] + _rms(hout, prm.post[...])


def _mixer_kernel(tiles_per_seq, sinks_ref, x_ref, xn_ref, pos_ref, posn_ref, invf_ref, place_ref, pre_ref,
                  win_ref, wpool_ref, pscale_ref, gpool_ref, gattn_ref, wout_ref, post_ref, o_ref,
                  band_ref, hn_ref, *scratch):
    T = MIX_TOKENS
    n_set = len(scratch) // 2
    set0, set1 = _TileScratch(*scratch[:n_set]), _TileScratch(*scratch[n_set:])
    prm = _Params(sinks_ref, invf_ref, place_ref, pre_ref, win_ref, wpool_ref, pscale_ref, gpool_ref,
                  gattn_ref, wout_ref, post_ref, band_ref, hn_ref)
    g = pl.program_id(0)
    s_even = (2 * g) % tiles_per_seq

    @pl.when(g == 0)
    def _():
        qi = lax.broadcasted_iota(jnp.int32, (2 * BLOCK, 2 * BLOCK), 0) % BLOCK
        kj = lax.broadcasted_iota(jnp.int32, (2 * BLOCK, 2 * BLOCK), 1)
        band = (kj > qi) & (kj <= qi + BLOCK)
        band_ref[0] = band.astype(_F32)
        band_ref[1] = (band & (kj >= BLOCK)).astype(_F32)
        _project(x_ref[0:T, :], pos_ref[:, 0:T], prm, set0, None, True)

    _project(x_ref[T:2 * T, :], pos_ref[:, T:2 * T], prm, set1, set0, False)
    o_ref[0:T, :] = _mix(x_ref, 0, prm, set0, s_even)

    next_start = (2 * g + 2) % tiles_per_seq == 0
    _project(xn_ref[...], posn_ref[...], prm, set0, set1, next_start)
    o_ref[T:2 * T, :] = _mix(x_ref, T, prm, set1, s_even + 1)


def _mixer(x, positions, mix_pre, w_in, w_pool, pool_scale, sinks, g_pool, g_attn, w_out, mix_post):
    B, S, _ = x.shape
    T = MIX_TOKENS
    n_tiles = B * S // T
    assert S % (2 * T) == 0
    half = ROT_DIM // 2
    inv_freq = ROPE_THETA ** (-jnp.arange(0, ROT_DIM, 2, dtype=_F32) / ROT_DIM)
    dim = jnp.arange(LANES) % HEAD_DIM
    freq = jnp.arange(half)[:, None]
    cos_place = ((dim[None, :] < ROT_DIM) & (dim[None, :] % half == freq)).astype(_F32)
    sin_place = cos_place * jnp.where(dim < half, -1.0, 1.0)[None, :]
    zeros = jnp.zeros_like(cos_place)
    place = jnp.concatenate([jnp.concatenate([cos_place, zeros], axis=1),
                             jnp.concatenate([zeros, sin_place], axis=1)], axis=0)
    place = jnp.tile(place, (3, 1)).astype(_BF16)

    resident = dict(pipeline_mode=pl.Buffered(1))
    full = lambda shape: pl.BlockSpec(shape, lambda g, sinks_ref: (0,) * len(shape), **resident)
    next_tile = lambda g, sinks_ref: jnp.minimum(2 * g + 2, n_tiles - 1)
    grid_spec = pltpu.PrefetchScalarGridSpec(
        num_scalar_prefetch=1,
        grid=(n_tiles // 2,),
        in_specs=[
            pl.BlockSpec((2 * T, D_MODEL), lambda g, sinks_ref: (g, 0)),
            pl.BlockSpec((T, D_MODEL), lambda g, sinks_ref: (next_tile(g, sinks_ref), 0)),
            pl.BlockSpec((1, 2 * T), lambda g, sinks_ref: (0, g)),
            pl.BlockSpec((1, T), lambda g, sinks_ref: (0, next_tile(g, sinks_ref))),
            full((half, 1)),
            full((3 * ROT_DIM, 2 * LANES)),
            full((1, D_MODEL)),
            full((D_MODEL, IN_WIDTH)),
            full((len(POOL_WINDOWS), POOL_GROUP, POOL_GROUP)),
            full((1, POOL_WIDTH)),
            full((1, POOL_WIDTH)),
            full((1, ATTN_WIDTH)),
            full((MIX_WIDTH, D_MODEL)),
            full((1, D_MODEL)),
        ],
        out_specs=pl.BlockSpec((2 * T, D_MODEL), lambda g, sinks_ref: (g, 0)),
        scratch_shapes=[pltpu.VMEM((2, 2 * BLOCK, 2 * BLOCK), _F32), pltpu.VMEM((T, D_MODEL), _BF16)]
        + _tile_scratch_shapes(T) + _tile_scratch_shapes(T),
    )
    out = pl.pallas_call(
        functools.partial(_mixer_kernel, S // T),
        name="mixer",
        grid_spec=grid_spec,
        out_shape=jax.ShapeDtypeStruct((B * S, D_MODEL), _F32),
        compiler_params=pltpu.CompilerParams(
            dimension_semantics=("arbitrary",), vmem_limit_bytes=MIX_VMEM_BYTES),
    )(sinks.astype(_F32), x.reshape(B * S, D_MODEL), x.reshape(B * S, D_MODEL),
      positions.reshape(1, B * S), positions.reshape(1, B * S), inv_freq.reshape(half, 1), place,
      mix_pre.reshape(1, D_MODEL), w_in, w_pool.astype(_BF16), pool_scale.reshape(1, POOL_WIDTH),
      g_pool.reshape(1, POOL_WIDTH), g_attn.reshape(1, ATTN_WIDTH), w_out, mix_post.reshape(1, D_MODEL))
    return out.reshape(B, S, D_MODEL)


def kernel(x, positions, ffn1_pre, ffn1_w_gu, ffn1_w_down, ffn1_post, mix_pre, w_in, w_pool, pool_scale, sinks, g_pool, g_attn, w_out, mix_post, ffn2_pre, ffn2_w_gu, ffn2_w_down, ffn2_post):
    B, S, D = x.shape
    for l in range(ffn1_pre.shape[0]):
        x, (w_in_b, w_out_b, w_gu2_b, w_down2_b) = _ffn(
            x.reshape(B * S, D), ffn1_pre[l], ffn1_w_gu[l].astype(_BF16), ffn1_w_down[l].astype(_BF16),
            ffn1_post[l], later_weights=((w_in, l), (w_out, l), (ffn2_w_gu, l), (ffn2_w_down, l)))
        x = _mixer(x.reshape(B, S, D), positions, mix_pre[l], w_in_b, w_pool[l], pool_scale[l], sinks[l],
                   g_pool[l], g_attn[l], w_out_b, mix_post[l])
        x, _ = _ffn(x.reshape(B * S, D), ffn2_pre[l], w_gu2_b, w_down2_b, ffn2_post[l])
        x = x.reshape(B, S, D)
    return x
```

```python
import collections
import functools

import jax
import jax.numpy as jnp
from jax import lax
from jax.experimental import pallas as pl
from jax.experimental.pallas import tpu as pltpu

D_MODEL = 1024
D_FF = 2816
POOL_WINDOWS = (2, 4, 8, 16)
POOL_WIDTH = 512
POOL_GROUP = 128
HEAD_DIM = 64
N_HEADS = 8
N_KV_HEADS = 2
ATTN_WIDTH = 512
KV_WIDTH = 128
BLOCK = 128
ROPE_THETA = 500000.0
ROT_DIM = 16
MIX_WIDTH = 1024
IN_WIDTH = 1280
EPS = 1e-6
NEG_INF = -1e30
LOG2E = 1.4426950408889634

LANES = 128
BF16_SUBLANES = 16
POOL_HALO = 16

FFN_TOKENS = 1024
FFN_CHUNK = 256
MIX_TOKENS = 512
FFN_VMEM_BYTES = 56 * 1024 * 1024
MIX_VMEM_BYTES = 56 * 1024 * 1024

_F32 = jnp.float32
_BF16 = jnp.bfloat16


def _rms(x, g):
    return x * lax.rsqrt(jnp.mean(x * x, axis=-1, keepdims=True) + EPS) * g


def _ffn_kernel(n_cast, x_ref, pre_ref, wgu_ref, wd_ref, post_ref, *rest):
    cast_in, o_ref, cast_out = rest[:n_cast], rest[n_cast], rest[n_cast + 1:2 * n_cast + 1]
    xn_ref, act_ref = rest[2 * n_cast + 1:]
    for src, dst in zip(cast_in, cast_out):
        dst[...] = src[...].astype(_BF16)
    T = FFN_TOKENS
    halves = [slice(i * T // 2, (i + 1) * T // 2) for i in range(2)]

    def gate_up(rows, c):
        lo = c * FFN_CHUNK
        xn = xn_ref[rows, :]
        g = jnp.dot(xn, wgu_ref[:, lo:lo + FFN_CHUNK], preferred_element_type=_F32)
        u = jnp.dot(xn, wgu_ref[:, D_FF + lo:D_FF + lo + FFN_CHUNK], preferred_element_type=_F32)
        act_ref[rows, lo:lo + FFN_CHUNK] = (g * jax.nn.sigmoid(g) * u).astype(_BF16)

    for rows in halves:
        xn_ref[rows, :] = _rms(x_ref[rows, :], pre_ref[...]).astype(_BF16)
    for rows in halves:
        gate_up(rows, 0)
    for c in range(1, D_FF // FFN_CHUNK):
        gate_up(slice(0, T), c)
    for rows in halves:
        h = jnp.dot(act_ref[rows, :], wd_ref[...], preferred_element_type=_F32)
        o_ref[rows, :] = x_ref[rows, :] + _rms(h, post_ref[...])


def _ffn(x2d, pre, w_gu, w_down, post, later_weights=()):
    n_tok = x2d.shape[0]
    step_tokens = FFN_TOKENS
    n_steps = n_tok // step_tokens
    const = lambda i: (0, 0)
    resident = dict(pipeline_mode=pl.Buffered(1))
    slab_in, slab_out, slab_shapes = [], [], []
    for w, layer in later_weights:
        _, n_rows, n_cols = w.shape
        slab = next(r for r in range(BF16_SUBLANES, n_rows + 1, BF16_SUBLANES)
                    if n_rows % r == 0 and n_rows // r <= n_steps)
        last = n_rows // slab - 1
        slab_in.append(pl.BlockSpec((None, slab, n_cols), lambda i, layer=layer, last=last:
                                    (layer, jnp.minimum(i, last), 0)))
        slab_out.append(pl.BlockSpec((slab, n_cols), lambda i, last=last: (jnp.minimum(i, last), 0)))
        slab_shapes.append(jax.ShapeDtypeStruct((n_rows, n_cols), _BF16))
    outs = pl.pallas_call(
        functools.partial(_ffn_kernel, len(later_weights)),
        name="ffn",
        grid=(n_steps,),
        in_specs=[
            pl.BlockSpec((step_tokens, D_MODEL), lambda i: (i, 0)),
            pl.BlockSpec((1, D_MODEL), const, **resident),
            pl.BlockSpec((D_MODEL, 2 * D_FF), const, **resident),
            pl.BlockSpec((D_FF, D_MODEL), const, **resident),
            pl.BlockSpec((1, D_MODEL), const, **resident),
        ] + slab_in,
        out_specs=[pl.BlockSpec((step_tokens, D_MODEL), lambda i: (i, 0))] + slab_out,
        out_shape=[jax.ShapeDtypeStruct(x2d.shape, _F32)] + slab_shapes,
        scratch_shapes=[pltpu.VMEM((FFN_TOKENS, D_MODEL), _BF16), pltpu.VMEM((FFN_TOKENS, D_FF), _BF16)],
        compiler_params=pltpu.CompilerParams(
            dimension_semantics=("arbitrary",), vmem_limit_bytes=FFN_VMEM_BYTES),
    )(x2d, pre.reshape(1, D_MODEL), w_gu, w_down, (0.5 * post).reshape(1, D_MODEL),
      *[w for w, _ in later_weights])
    return outs[0], list(outs[1:])


_TileScratch = collections.namedtuple("_TileScratch", "ubuf kbuf vbuf qbuf attn sbuf pbuf ibuf")
_Params = collections.namedtuple(
    "_Params", "sinks invf place pre win wpool pscale gpool gattn wout post band hn")


def _tile_scratch_shapes(T):
    return [
        pltpu.VMEM((T + POOL_HALO, POOL_WIDTH), _F32),
        pltpu.VMEM((4, T + BLOCK, LANES), _BF16),
        pltpu.VMEM((4, T + BLOCK, LANES), _BF16),
        pltpu.VMEM((ATTN_WIDTH // LANES, T, LANES), _BF16),
        pltpu.VMEM((ATTN_WIDTH // LANES, T, LANES), _F32),
        pltpu.VMEM((2, 2, 2 * BLOCK, 2 * BLOCK), _F32),
        pltpu.VMEM((2, 2, 2 * BLOCK, 2 * BLOCK), _BF16),
        pltpu.VMEM((2, 2, 2 * BLOCK, LANES), _F32),
    ]


def _rope(t, cos_t, sin_t, lo_lanes):
    partner = jnp.where(lo_lanes, pltpu.roll(t, LANES - ROT_DIM // 2, 1), pltpu.roll(t, ROT_DIM // 2, 1))
    return t * cos_t + partner * sin_t


def _project(x, pos_row, prm, cur, prev, is_start):
    T = x.shape[0]
    prm.hn[...] = _rms(x, prm.pre[...]).astype(_BF16)

    def win_cols(lo, n):
        return jnp.dot(prm.hn[...], prm.win[:, lo:lo + n], preferred_element_type=_F32)

    if prev is None:
        cur.ubuf[0:POOL_HALO, :] = jnp.zeros((POOL_HALO, POOL_WIDTH), _F32)
        cur.kbuf[:, 0:BLOCK, :] = jnp.zeros((4, BLOCK, LANES), _BF16)
        cur.vbuf[:, 0:BLOCK, :] = jnp.zeros((4, BLOCK, LANES), _BF16)
    else:
        cur.ubuf[0:POOL_HALO, :] = jnp.where(is_start, 0.0, prev.ubuf[T:T + POOL_HALO, :])
        cur.kbuf[:, 0:BLOCK, :] = jnp.where(is_start, 0.0, prev.kbuf[:, T:T + BLOCK, :]).astype(_BF16)
        cur.vbuf[:, 0:BLOCK, :] = jnp.where(is_start, 0.0, prev.vbuf[:, T:T + BLOCK, :]).astype(_BF16)
    for lo in range(0, POOL_WIDTH, 2 * LANES):
        cur.ubuf[POOL_HALO:POOL_HALO + T, lo:lo + 2 * LANES] = win_cols(lo, 2 * LANES)

    lane = lax.broadcasted_iota(jnp.int32, (1, LANES), 1)
    dim = lane % HEAD_DIM
    lo_lanes = dim < ROT_DIM // 2
    ang = pos_row.astype(_F32) * prm.invf[...]
    cs = jnp.concatenate([jnp.cos(ang), jnp.sin(ang)], axis=0)
    cs_hi = cs.astype(_BF16)
    rem = cs - cs_hi.astype(_F32)
    cs_mid = rem.astype(_BF16)
    cs_lo = (rem - cs_mid.astype(_F32)).astype(_BF16)
    cs3 = jnp.concatenate([cs_hi, cs_mid, cs_lo], axis=0)
    tables = lax.dot_general(cs3, prm.place[...], (((0,), (0,)), ((), ())), preferred_element_type=_F32)
    cos_t = jnp.where(dim < ROT_DIM, tables[:, :LANES], 1.0)
    sin_t = tables[:, LANES:]

    scale = HEAD_DIM ** -0.5 * LOG2E
    for p in range(0, ATTN_WIDTH // LANES, 2):
        q2 = win_cols(POOL_WIDTH + p * LANES, 2 * LANES)
        for j in range(2):
            qp = q2[:, j * LANES:(j + 1) * LANES]
            cur.qbuf[p + j] = (_rope(qp, cos_t, sin_t, lo_lanes) * scale).astype(_BF16)

    first_half = lane < HEAD_DIM
    kv_cols = win_cols(MIX_WIDTH, 2 * KV_WIDTH)
    k = _rope(kv_cols[:, :KV_WIDTH], cos_t, sin_t, lo_lanes)
    v = kv_cols[:, KV_WIDTH:]
    for buf, t in ((cur.kbuf, k), (cur.vbuf, v)):
        t_sw = pltpu.roll(t, HEAD_DIM, 1)
        variants = (jnp.where(first_half, t, 0.0), jnp.where(first_half, 0.0, t_sw),
                    jnp.where(first_half, t_sw, 0.0), jnp.where(first_half, 0.0, t))
        for idx, var in enumerate(variants):
            buf[idx, BLOCK:BLOCK + T, :] = var.astype(_BF16)


def _mix(x_ref, row0, prm, cur, s_tile):
    T = MIX_TOKENS
    is_start = s_tile == 0

    row = lax.broadcasted_iota(jnp.int32, (T, 1), 0)
    seen = s_tile * T + row + 1
    pooled = []
    for g, w in enumerate(POOL_WINDOWS):
        lo = g * POOL_GROUP
        win_sum = cur.ubuf[:, lo:lo + POOL_GROUP]
        span = 1
        while span < w:
            win_sum = win_sum + pltpu.roll(win_sum, span, 0)
            span *= 2
        cnt = jnp.minimum(seen, w).astype(_F32)
        d = win_sum[POOL_HALO:, :] / cnt - cur.ubuf[POOL_HALO:POOL_HALO + T, lo:lo + POOL_GROUP]
        pooled.append(jnp.dot(d.astype(_BF16), prm.wpool[g], preferred_element_type=_F32))
    pool_out = jnp.concatenate(pooled, axis=1) * prm.pscale[...]

    upper_rows = lax.broadcasted_iota(jnp.int32, (2 * BLOCK, 1), 0) < BLOCK
    nt = (((1,), (1,)), ((), ()))
    n_units = N_KV_HEADS * T // BLOCK
    ones = jnp.ones((2 * BLOCK, LANES), _BF16)

    def rows(u, n):
        return pl.ds((u // N_KV_HEADS) * BLOCK, n)

    def scores(u):
        kv, slot = u % N_KV_HEADS, u % 2
        q2 = jnp.concatenate([cur.qbuf[2 * kv, rows(u, BLOCK), :], cur.qbuf[2 * kv + 1, rows(u, BLOCK), :]],
                             axis=0)
        valid = prm.band[jnp.where(is_start, 1, 0) if u < N_KV_HEADS else 0] > 0.5
        for e in range(2):
            s = lax.dot_general(q2, cur.kbuf[2 * kv + e, rows(u, 2 * BLOCK), :], nt, preferred_element_type=_F32)
            cur.sbuf[slot, e] = jnp.where(valid, s, NEG_INF)

    def softmax(u):
        kv, slot = u % N_KV_HEADS, u % 2
        for e in range(2):
            head_a = 4 * kv + e
            sink = jnp.where(upper_rows, prm.sinks[head_a] * LOG2E, prm.sinks[head_a + 2] * LOG2E)
            s = cur.sbuf[slot, e]
            m = jnp.maximum(jnp.max(s, axis=1, keepdims=True), sink)
            cur.pbuf[slot, e] = jnp.exp2(s - m).astype(_BF16)
            cur.ibuf[slot, e] = jnp.broadcast_to(jnp.exp2(sink - m), (2 * BLOCK, LANES))

    def weighted(u):
        kv, slot = u % N_KV_HEADS, u % 2
        out2 = None
        for e in range(2):
            v_ones = jnp.concatenate([cur.vbuf[2 * kv + e, rows(u, 2 * BLOCK), :], ones], axis=1)
            pv = jnp.dot(cur.pbuf[slot, e], v_ones, preferred_element_type=_F32)
            pv = pv[:, :LANES] / (pv[:, LANES:] + cur.ibuf[slot, e])
            out2 = pv if out2 is None else out2 + pv
        cur.attn[2 * kv, rows(u, BLOCK), :] = out2[:BLOCK]
        cur.attn[2 * kv + 1, rows(u, BLOCK), :] = out2[BLOCK:]

    scores(0)
    scores(1)
    softmax(0)
    for u in range(2, n_units):
        weighted(u - 2)
        softmax(u - 1)
        scores(u)
    softmax(n_units - 1)
    weighted(n_units - 2)
    weighted(n_units - 1)

    y_pool = _rms(pool_out, prm.gpool[...]).astype(_BF16)
    attn = jnp.concatenate([cur.attn[p] for p in range(ATTN_WIDTH // LANES)], axis=1)
    y_attn = _rms(attn, prm.gattn[...]).astype(_BF16)
    hout = (jnp.dot(y_pool, prm.wout[0:POOL_WIDTH, :], preferred_element_type=_F32)
            + jnp.dot(y_attn, prm.wout[POOL_WIDTH:, :], preferred_element_type=_F32))
    return x_ref[row0:row0 + T, :] + _rms(hout, prm.post[...])


def _mixer_kernel(tiles_per_seq, sinks_ref, x_ref, xn_ref, pos_ref, posn_ref, invf_ref, place_ref, pre_ref,
                  win_ref, wpool_ref, pscale_ref, gpool_ref, gattn_ref, wout_ref, post_ref, o_ref,
                  band_ref, hn_ref, *scratch):
    T = MIX_TOKENS
    n_set = len(scratch) // 2
    set0, set1 = _TileScratch(*scratch[:n_set]), _TileScratch(*scratch[n_set:])
    prm = _Params(sinks_ref, invf_ref, place_ref, pre_ref, win_ref, wpool_ref, pscale_ref, gpool_ref,
                  gattn_ref, wout_ref, post_ref, band_ref, hn_ref)
    g = pl.program_id(0)
    s_even = (2 * g) % tiles_per_seq

    @pl.when(g == 0)
    def _():
        qi = lax.broadcasted_iota(jnp.int32, (2 * BLOCK, 2 * BLOCK), 0) % BLOCK
        kj = lax.broadcasted_iota(jnp.int32, (2 * BLOCK, 2 * BLOCK), 1)
        band = (kj > qi) & (kj <= qi + BLOCK)
        band_ref[0] = band.astype(_F32)
        band_ref[1] = (band & (kj >= BLOCK)).astype(_F32)
        _project(x_ref[0:T, :], pos_ref[:, 0:T], prm, set0, None, True)

    _project(x_ref[T:2 * T, :], pos_ref[:, T:2 * T], prm, set1, set0, False)
    o_ref[0:T, :] = _mix(x_ref, 0, prm, set0, s_even)

    next_start = (2 * g + 2) % tiles_per_seq == 0
    _project(xn_ref[...], posn_ref[...], prm, set0, set1, next_start)
    o_ref[T:2 * T, :] = _mix(x_ref, T, prm, set1, s_even + 1)


def _mixer(x, positions, mix_pre, w_in, w_pool, pool_scale, sinks, g_pool, g_attn, w_out, mix_post):
    B, S, _ = x.shape
    T = MIX_TOKENS
    n_tiles = B * S // T
    assert S % (2 * T) == 0
    half = ROT_DIM // 2
    inv_freq = ROPE_THETA ** (-jnp.arange(0, ROT_DIM, 2, dtype=_F32) / ROT_DIM)
    dim = jnp.arange(LANES) % HEAD_DIM
    freq = jnp.arange(half)[:, None]
    cos_place = ((dim[None, :] < ROT_DIM) & (dim[None, :] % half == freq)).astype(_F32)
    sin_place = cos_place * jnp.where(dim < half, -1.0, 1.0)[None, :]
    zeros = jnp.zeros_like(cos_place)
    place = jnp.concatenate([jnp.concatenate([cos_place, zeros], axis=1),
                             jnp.concatenate([zeros, sin_place], axis=1)], axis=0)
    place = jnp.tile(place, (3, 1)).astype(_BF16)

    resident = dict(pipeline_mode=pl.Buffered(1))
    full = lambda shape: pl.BlockSpec(shape, lambda g, sinks_ref: (0,) * len(shape), **resident)
    next_tile = lambda g, sinks_ref: jnp.minimum(2 * g + 2, n_tiles - 1)
    grid_spec = pltpu.PrefetchScalarGridSpec(
        num_scalar_prefetch=1,
        grid=(n_tiles // 2,),
        in_specs=[
            pl.BlockSpec((2 * T, D_MODEL), lambda g, sinks_ref: (g, 0)),
            pl.BlockSpec((T, D_MODEL), lambda g, sinks_ref: (next_tile(g, sinks_ref), 0)),
            pl.BlockSpec((1, 2 * T), lambda g, sinks_ref: (0, g)),
            pl.BlockSpec((1, T), lambda g, sinks_ref: (0, next_tile(g, sinks_ref))),
            full((half, 1)),
            full((3 * ROT_DIM, 2 * LANES)),
            full((1, D_MODEL)),
            full((D_MODEL, IN_WIDTH)),
            full((len(POOL_WINDOWS), POOL_GROUP, POOL_GROUP)),
            full((1, POOL_WIDTH)),
            full((1, POOL_WIDTH)),
            full((1, ATTN_WIDTH)),
            full((MIX_WIDTH, D_MODEL)),
            full((1, D_MODEL)),
        ],
        out_specs=pl.BlockSpec((2 * T, D_MODEL), lambda g, sinks_ref: (g, 0)),
        scratch_shapes=[pltpu.VMEM((2, 2 * BLOCK, 2 * BLOCK), _F32), pltpu.VMEM((T, D_MODEL), _BF16)]
        + _tile_scratch_shapes(T) + _tile_scratch_shapes(T),
    )
    out = pl.pallas_call(
        functools.partial(_mixer_kernel, S // T),
        name="mixer",
        grid_spec=grid_spec,
        out_shape=jax.ShapeDtypeStruct((B * S, D_MODEL), _F32),
        compiler_params=pltpu.CompilerParams(
            dimension_semantics=("arbitrary",), vmem_limit_bytes=MIX_VMEM_BYTES),
    )(sinks.astype(_F32), x.reshape(B * S, D_MODEL), x.reshape(B * S, D_MODEL),
      positions.reshape(1, B * S), positions.reshape(1, B * S), inv_freq.reshape(half, 1), place,
      mix_pre.reshape(1, D_MODEL), w_in, w_pool.astype(_BF16), pool_scale.reshape(1, POOL_WIDTH),
      g_pool.reshape(1, POOL_WIDTH), g_attn.reshape(1, ATTN_WIDTH), w_out, mix_post.reshape(1, D_MODEL))
    return out.reshape(B, S, D_MODEL)


def kernel(x, positions, ffn1_pre, ffn1_w_gu, ffn1_w_down, ffn1_post, mix_pre, w_in, w_pool, pool_scale, sinks, g_pool, g_attn, w_out, mix_post, ffn2_pre, ffn2_w_gu, ffn2_w_down, ffn2_post):
    B, S, D = x.shape
    for l in range(ffn1_pre.shape[0]):
        x, (w_in_b, w_out_b, w_gu2_b, w_down2_b) = _ffn(
            x.reshape(B * S, D), ffn1_pre[l], ffn1_w_gu[l].astype(_BF16), ffn1_w_down[l].astype(_BF16),
            ffn1_post[l], later_weights=((w_in, l), (w_out, l), (ffn2_w_gu, l), (ffn2_w_down, l)))
        x = _mixer(x.reshape(B, S, D), positions, mix_pre[l], w_in_b, w_pool[l], pool_scale[l], sinks[l],
                   g_pool[l], g_attn[l], w_out_b, mix_post[l])
        x, _ = _ffn(x.reshape(B * S, D), ffn2_pre[l], w_gu2_b, w_down2_b, ffn2_post[l])
        x = x.reshape(B, S, D)
    return x
```

```python
import collections
import functools

import jax
import jax.numpy as jnp
from jax import lax
from jax.experimental import pallas as pl
from jax.experimental.pallas import tpu as pltpu

D_MODEL = 1024
D_FF = 2816
POOL_WINDOWS = (2, 4, 8, 16)
POOL_WIDTH = 512
POOL_GROUP = 128
HEAD_DIM = 64
N_HEADS = 8
N_KV_HEADS = 2
ATTN_WIDTH = 512
KV_WIDTH = 128
BLOCK = 128
ROPE_THETA = 500000.0
ROT_DIM = 16
MIX_WIDTH = 1024
IN_WIDTH = 1280
EPS = 1e-6
NEG_INF = -1e30
LOG2E = 1.4426950408889634

LANES = 128
BF16_SUBLANES = 16
POOL_HALO = 16

FFN_TOKENS = 1024
FFN_CHUNK = 256
MIX_TOKENS = 512
FFN_VMEM_BYTES = 56 * 1024 * 1024
MIX_VMEM_BYTES = 56 * 1024 * 1024

_F32 = jnp.float32
_BF16 = jnp.bfloat16


def _rms(x, g):
    return x * lax.rsqrt(jnp.mean(x * x, axis=-1, keepdims=True) + EPS) * g


def _ffn_kernel(n_cast, x_ref, pre_ref, wgu_ref, wd_ref, post_ref, *rest):
    cast_in, o_ref, cast_out = rest[:n_cast], rest[n_cast], rest[n_cast + 1:2 * n_cast + 1]
    xn_ref, act_ref = rest[2 * n_cast + 1:]
    for src, dst in zip(cast_in, cast_out):
        dst[...] = src[...].astype(_BF16)
    x = x_ref[...]
    xn_ref[...] = _rms(x, pre_ref[...]).astype(_BF16)
    for c in range(D_FF // FFN_CHUNK):
        lo = c * FFN_CHUNK
        xn = xn_ref[...]
        g = jnp.dot(xn, wgu_ref[:, lo:lo + FFN_CHUNK], preferred_element_type=_F32)
        u = jnp.dot(xn, wgu_ref[:, D_FF + lo:D_FF + lo + FFN_CHUNK], preferred_element_type=_F32)
        act_ref[:, lo:lo + FFN_CHUNK] = (g * jax.nn.sigmoid(g) * u).astype(_BF16)
    h = jnp.dot(act_ref[...], wd_ref[...], preferred_element_type=_F32)
    o_ref[...] = x + _rms(h, post_ref[...])


def _ffn(x2d, pre, w_gu, w_down, post, later_weights=()):
    n_tok = x2d.shape[0]
    step_tokens = FFN_TOKENS
    n_steps = n_tok // step_tokens
    const = lambda i: (0, 0)
    resident = dict(pipeline_mode=pl.Buffered(1))
    slab_in, slab_out, slab_shapes = [], [], []
    for w, layer in later_weights:
        _, n_rows, n_cols = w.shape
        slab = next(r for r in range(BF16_SUBLANES, n_rows + 1, BF16_SUBLANES)
                    if n_rows % r == 0 and n_rows // r <= n_steps)
        last = n_rows // slab - 1
        slab_in.append(pl.BlockSpec((None, slab, n_cols), lambda i, layer=layer, last=last:
                                    (layer, jnp.minimum(i, last), 0)))
        slab_out.append(pl.BlockSpec((slab, n_cols), lambda i, last=last: (jnp.minimum(i, last), 0)))
        slab_shapes.append(jax.ShapeDtypeStruct((n_rows, n_cols), _BF16))
    outs = pl.pallas_call(
        functools.partial(_ffn_kernel, len(later_weights)),
        name="ffn",
        grid=(n_steps,),
        in_specs=[
            pl.BlockSpec((step_tokens, D_MODEL), lambda i: (i, 0)),
            pl.BlockSpec((1, D_MODEL), const, **resident),
            pl.BlockSpec((D_MODEL, 2 * D_FF), const, **resident),
            pl.BlockSpec((D_FF, D_MODEL), const, **resident),
            pl.BlockSpec((1, D_MODEL), const, **resident),
        ] + slab_in,
        out_specs=[pl.BlockSpec((step_tokens, D_MODEL), lambda i: (i, 0))] + slab_out,
        out_shape=[jax.ShapeDtypeStruct(x2d.shape, _F32)] + slab_shapes,
        scratch_shapes=[pltpu.VMEM((FFN_TOKENS, D_MODEL), _BF16), pltpu.VMEM((FFN_TOKENS, D_FF), _BF16)],
        compiler_params=pltpu.CompilerParams(
            dimension_semantics=("arbitrary",), vmem_limit_bytes=FFN_VMEM_BYTES),
    )(x2d, pre.reshape(1, D_MODEL), w_gu, w_down, (0.5 * post).reshape(1, D_MODEL),
      *[w for w, _ in later_weights])
    return outs[0], list(outs[1:])


_TileScratch = collections.namedtuple("_TileScratch", "ubuf kbuf vbuf qbuf attn sbuf pbuf ibuf")
_Params = collections.namedtuple(
    "_Params", "sinks invf place pre win wpool pscale gpool gattn wout post band eye hn")


def _tile_scratch_shapes(T):
    return [
        pltpu.VMEM((T + POOL_HALO, POOL_WIDTH), _F32),
        pltpu.VMEM((4, T + BLOCK, LANES), _BF16),
        pltpu.VMEM((4, T + BLOCK, LANES), _BF16),
        pltpu.VMEM((ATTN_WIDTH // LANES, T, LANES), _BF16),
        pltpu.VMEM((ATTN_WIDTH // LANES, T, LANES), _F32),
        pltpu.VMEM((2, 2, 2 * BLOCK, 2 * BLOCK), _F32),
        pltpu.VMEM((2, 2, 2 * BLOCK, 2 * BLOCK), _BF16),
        pltpu.VMEM((2, 2, 2 * BLOCK, LANES), _F32),
    ]


def _rope(t, cos_t, sin_t, lo_lanes):
    partner = jnp.where(lo_lanes, pltpu.roll(t, LANES - ROT_DIM // 2, 1), pltpu.roll(t, ROT_DIM // 2, 1))
    return t * cos_t + partner * sin_t


def _interleave(a, b):
    out, ia, ib = [], 0, 0
    while ia < len(a) or ib < len(b):
        if ib >= len(b) or (ia < len(a) and ia * len(b) <= ib * len(a)):
            out.append(a[ia])
            ia += 1
        else:
            out.append(b[ib])
            ib += 1
    return out


def _run(tasks):
    for task in tasks:
        task()


def _project_tasks(load_x, load_pos, prm, cur, prev, is_start):
    T = MIX_TOKENS
    st = {}
    lane = lax.broadcasted_iota(jnp.int32, (1, LANES), 1)
    dim = lane % HEAD_DIM
    lo_lanes = dim < ROT_DIM // 2

    def win_cols(lo, n):
        return jnp.dot(prm.hn[...], prm.win[:, lo:lo + n], preferred_element_type=_F32)

    def norm_and_carry():
        prm.hn[...] = _rms(load_x(), prm.pre[...]).astype(_BF16)
        if prev is None:
            cur.ubuf[0:POOL_HALO, :] = jnp.zeros((POOL_HALO, POOL_WIDTH), _F32)
            cur.kbuf[:, 0:BLOCK, :] = jnp.zeros((4, BLOCK, LANES), _BF16)
            cur.vbuf[:, 0:BLOCK, :] = jnp.zeros((4, BLOCK, LANES), _BF16)
        else:
            cur.ubuf[0:POOL_HALO, :] = jnp.where(is_start, 0.0, prev.ubuf[T:T + POOL_HALO, :])
            cur.kbuf[:, 0:BLOCK, :] = jnp.where(is_start, 0.0, prev.kbuf[:, T:T + BLOCK, :]).astype(_BF16)
            cur.vbuf[:, 0:BLOCK, :] = jnp.where(is_start, 0.0, prev.vbuf[:, T:T + BLOCK, :]).astype(_BF16)

    def pool_cols(lo):
        cur.ubuf[POOL_HALO:POOL_HALO + T, lo:lo + 2 * LANES] = win_cols(lo, 2 * LANES)

    def rotary_tables():
        ang = load_pos().astype(_F32) * prm.invf[...]
        cs = jnp.concatenate([jnp.cos(ang), jnp.sin(ang)], axis=0)
        cs_hi = cs.astype(_BF16)
        rem = cs - cs_hi.astype(_F32)
        cs_mid = rem.astype(_BF16)
        cs_lo = (rem - cs_mid.astype(_F32)).astype(_BF16)
        cs3 = jnp.concatenate([cs_hi, cs_mid, cs_lo], axis=0)
        tables = lax.dot_general(cs3, prm.place[...], (((0,), (0,)), ((), ())), preferred_element_type=_F32)
        st["cos"] = jnp.where(dim < ROT_DIM, tables[:, :LANES], 1.0)
        st["sin"] = tables[:, LANES:]

    def query_cols(p):
        scale = HEAD_DIM ** -0.5 * LOG2E
        q2 = win_cols(POOL_WIDTH + p * LANES, 2 * LANES)
        for j in range(2):
            qp = q2[:, j * LANES:(j + 1) * LANES]
            cur.qbuf[p + j] = (_rope(qp, st["cos"], st["sin"], lo_lanes) * scale).astype(_BF16)

    def key_value_cols():
        first_half = lane < HEAD_DIM
        kv_cols = win_cols(MIX_WIDTH, 2 * KV_WIDTH)
        k = _rope(kv_cols[:, :KV_WIDTH], st["cos"], st["sin"], lo_lanes)
        v = kv_cols[:, KV_WIDTH:]
        for buf, t in ((cur.kbuf, k), (cur.vbuf, v)):
            t_sw = pltpu.roll(t, HEAD_DIM, 1)
            variants = (jnp.where(first_half, t, 0.0), jnp.where(first_half, 0.0, t_sw),
                        jnp.where(first_half, t_sw, 0.0), jnp.where(first_half, 0.0, t))
            for idx, var in enumerate(variants):
                buf[idx, BLOCK:BLOCK + T, :] = var.astype(_BF16)

    tasks = [norm_and_carry]
    tasks += [functools.partial(pool_cols, lo) for lo in range(0, POOL_WIDTH, 2 * LANES)]
    tasks += [rotary_tables]
    tasks += [functools.partial(query_cols, p) for p in range(0, ATTN_WIDTH // LANES, 2)]
    tasks += [key_value_cols]
    return tasks


def _attend_tasks(prm, cur, s_tile, st):
    T = MIX_TOKENS
    is_start = s_tile == 0
    st["pooled"] = []

    def pool_group(g, w):
        row = lax.broadcasted_iota(jnp.int32, (T, 1), 0)
        seen = s_tile * T + row + 1
        lo = g * POOL_GROUP
        win_sum = cur.ubuf[:, lo:lo + POOL_GROUP]
        span = 1
        while span < w:
            win_sum = win_sum + pltpu.roll(win_sum, span, 0)
            span *= 2
        cnt = jnp.minimum(seen, w).astype(_F32)
        d = win_sum[POOL_HALO:, :] / cnt - cur.ubuf[POOL_HALO:POOL_HALO + T, lo:lo + POOL_GROUP]
        st["pooled"].append(jnp.dot(d.astype(_BF16), prm.wpool[g], preferred_element_type=_F32))

    upper_rows = lax.broadcasted_iota(jnp.int32, (2 * BLOCK, 1), 0) < BLOCK
    nt = (((1,), (1,)), ((), ()))
    n_units = N_KV_HEADS * T // BLOCK
    ones = jnp.ones((2 * BLOCK, LANES), _BF16)

    def rows(u, n):
        return pl.ds((u // N_KV_HEADS) * BLOCK, n)

    def scores(u):
        kv, slot = u % N_KV_HEADS, u % 2
        q2 = jnp.concatenate([cur.qbuf[2 * kv, rows(u, BLOCK), :], cur.qbuf[2 * kv + 1, rows(u, BLOCK), :]],
                             axis=0)
        q2 = jnp.concatenate([q2, prm.eye[...]], axis=1)
        bias_t = prm.band[jnp.where(is_start, 1, 0) if u < N_KV_HEADS else 0]
        for e in range(2):
            keys = jnp.concatenate([cur.kbuf[2 * kv + e, rows(u, 2 * BLOCK), :], bias_t], axis=1)
            cur.sbuf[slot, e] = lax.dot_general(q2, keys, nt, preferred_element_type=_F32)

    def softmax(u):
        kv, slot = u % N_KV_HEADS, u % 2
        for e in range(2):
            head_a = 4 * kv + e
            sink = jnp.where(upper_rows, prm.sinks[head_a] * LOG2E, prm.sinks[head_a + 2] * LOG2E)
            s = cur.sbuf[slot, e]
            m = jnp.maximum(jnp.max(s, axis=1, keepdims=True), sink)
            cur.pbuf[slot, e] = jnp.exp2(s - m).astype(_BF16)
            cur.ibuf[slot, e] = jnp.broadcast_to(jnp.exp2(sink - m), (2 * BLOCK, LANES))

    def weighted(u):
        kv, slot = u % N_KV_HEADS, u % 2
        out2 = None
        for e in range(2):
            v_ones = jnp.concatenate([cur.vbuf[2 * kv + e, rows(u, 2 * BLOCK), :], ones], axis=1)
            pv = jnp.dot(cur.pbuf[slot, e], v_ones, preferred_element_type=_F32)
            pv = pv[:, :LANES] / (pv[:, LANES:] + cur.ibuf[slot, e])
            out2 = pv if out2 is None else out2 + pv
        cur.attn[2 * kv, rows(u, BLOCK), :] = out2[:BLOCK]
        cur.attn[2 * kv + 1, rows(u, BLOCK), :] = out2[BLOCK:]

    def step(u):
        if 0 <= u - 2 < n_units:
            weighted(u - 2)
        if 0 <= u - 1 < n_units:
            softmax(u - 1)
        if u < n_units:
            scores(u)

    tasks = [functools.partial(pool_group, g, w) for g, w in enumerate(POOL_WINDOWS)]
    tasks += [functools.partial(step, u) for u in range(n_units + 2)]
    return tasks


def _output_tasks(x_ref, o_ref, row0, prm, cur, st):
    T = MIX_TOKENS
    st["hout"] = []

    def group_norms():
        pool_out = jnp.concatenate(st["pooled"], axis=1) * prm.pscale[...]
        st["y_pool"] = _rms(pool_out, prm.gpool[...]).astype(_BF16)
        attn = jnp.concatenate([cur.attn[p] for p in range(ATTN_WIDTH // LANES)], axis=1)
        st["y_attn"] = _rms(attn, prm.gattn[...]).astype(_BF16)

    def out_cols(lo):
        st["hout"].append(
            jnp.dot(st["y_pool"], prm.wout[0:POOL_WIDTH, lo:lo + 2 * LANES], preferred_element_type=_F32)
            + jnp.dot(st["y_attn"], prm.wout[POOL_WIDTH:, lo:lo + 2 * LANES], preferred_element_type=_F32))

    def finish():
        hout = jnp.concatenate(st["hout"], axis=1)
        o_ref[row0:row0 + T, :] = x_ref[row0:row0 + T, :] + _rms(hout, prm.post[...])

    return [group_norms] + [functools.partial(out_cols, lo) for lo in range(0, D_MODEL, 2 * LANES)] + [finish]


def _mixer_kernel(tiles_per_seq, sinks_ref, x_ref, xn_ref, pos_ref, posn_ref, invf_ref, place_ref, pre_ref,
                  win_ref, wpool_ref, pscale_ref, gpool_ref, gattn_ref, wout_ref, post_ref, o_ref,
                  band_ref, eye_ref, hn_ref, *scratch):
    T = MIX_TOKENS
    n_set = len(scratch) // 2
    set0, set1 = _TileScratch(*scratch[:n_set]), _TileScratch(*scratch[n_set:])
    prm = _Params(sinks_ref, invf_ref, place_ref, pre_ref, win_ref, wpool_ref, pscale_ref, gpool_ref,
                  gattn_ref, wout_ref, post_ref, band_ref, eye_ref, hn_ref)
    g = pl.program_id(0)
    s_even = (2 * g) % tiles_per_seq

    @pl.when(g == 0)
    def _():
        kj = lax.broadcasted_iota(jnp.int32, (2 * BLOCK, BLOCK), 0)
        qi = lax.broadcasted_iota(jnp.int32, (2 * BLOCK, BLOCK), 1)
        band = (kj > qi) & (kj <= qi + BLOCK)
        band_ref[0] = jnp.where(band, 0.0, NEG_INF).astype(_BF16)
        band_ref[1] = jnp.where(band & (kj >= BLOCK), 0.0, NEG_INF).astype(_BF16)
        eye_ref[...] = jnp.where(kj % BLOCK == qi, 1.0, 0.0).astype(_BF16)
        _run(_project_tasks(lambda: x_ref[0:T, :], lambda: pos_ref[:, 0:T], prm, set0, None, True))

    next_start = (2 * g + 2) % tiles_per_seq == 0
    st_a, st_b = {}, {}
    project_b = _project_tasks(lambda: x_ref[T:2 * T, :], lambda: pos_ref[:, T:2 * T], prm, set1, set0, False)
    project_next = _project_tasks(lambda: xn_ref[...], lambda: posn_ref[...], prm, set0, set1, next_start)
    _run(_interleave(project_b, _attend_tasks(prm, set0, s_even, st_a)))
    _run(_interleave(_output_tasks(x_ref, o_ref, 0, prm, set0, st_a) + project_next,
                     _attend_tasks(prm, set1, s_even + 1, st_b)))
    _run(_output_tasks(x_ref, o_ref, T, prm, set1, st_b))


def _mixer(x, positions, mix_pre, w_in, w_pool, pool_scale, sinks, g_pool, g_attn, w_out, mix_post):
    B, S, _ = x.shape
    T = MIX_TOKENS
    n_tiles = B * S // T
    assert S % (2 * T) == 0
    half = ROT_DIM // 2
    inv_freq = ROPE_THETA ** (-jnp.arange(0, ROT_DIM, 2, dtype=_F32) / ROT_DIM)
    dim = jnp.arange(LANES) % HEAD_DIM
    freq = jnp.arange(half)[:, None]
    cos_place = ((dim[None, :] < ROT_DIM) & (dim[None, :] % half == freq)).astype(_F32)
    sin_place = cos_place * jnp.where(dim < half, -1.0, 1.0)[None, :]
    zeros = jnp.zeros_like(cos_place)
    place = jnp.concatenate([jnp.concatenate([cos_place, zeros], axis=1),
                             jnp.concatenate([zeros, sin_place], axis=1)], axis=0)
    place = jnp.tile(place, (3, 1)).astype(_BF16)

    resident = dict(pipeline_mode=pl.Buffered(1))
    full = lambda shape: pl.BlockSpec(shape, lambda g, sinks_ref: (0,) * len(shape), **resident)
    next_tile = lambda g, sinks_ref: jnp.minimum(2 * g + 2, n_tiles - 1)
    grid_spec = pltpu.PrefetchScalarGridSpec(
        num_scalar_prefetch=1,
        grid=(n_tiles // 2,),
        in_specs=[
            pl.BlockSpec((2 * T, D_MODEL), lambda g, sinks_ref: (g, 0)),
            pl.BlockSpec((T, D_MODEL), lambda g, sinks_ref: (next_tile(g, sinks_ref), 0)),
            pl.BlockSpec((1, 2 * T), lambda g, sinks_ref: (0, g)),
            pl.BlockSpec((1, T), lambda g, sinks_ref: (0, next_tile(g, sinks_ref))),
            full((half, 1)),
            full((3 * ROT_DIM, 2 * LANES)),
            full((1, D_MODEL)),
            full((D_MODEL, IN_WIDTH)),
            full((len(POOL_WINDOWS), POOL_GROUP, POOL_GROUP)),
            full((1, POOL_WIDTH)),
            full((1, POOL_WIDTH)),
            full((1, ATTN_WIDTH)),
            full((MIX_WIDTH, D_MODEL)),
            full((1, D_MODEL)),
        ],
        out_specs=pl.BlockSpec((2 * T, D_MODEL), lambda g, sinks_ref: (g, 0)),
        scratch_shapes=[pltpu.VMEM((2, 2 * BLOCK, BLOCK), _BF16), pltpu.VMEM((2 * BLOCK, BLOCK), _BF16),
                        pltpu.VMEM((T, D_MODEL), _BF16)]
        + _tile_scratch_shapes(T) + _tile_scratch_shapes(T),
    )
    out = pl.pallas_call(
        functools.partial(_mixer_kernel, S // T),
        name="mixer",
        grid_spec=grid_spec,
        out_shape=jax.ShapeDtypeStruct((B * S, D_MODEL), _F32),
        compiler_params=pltpu.CompilerParams(
            dimension_semantics=("arbitrary",), vmem_limit_bytes=MIX_VMEM_BYTES),
    )(sinks.astype(_F32), x.reshape(B * S, D_MODEL), x.reshape(B * S, D_MODEL),
      positions.reshape(1, B * S), positions.reshape(1, B * S), inv_freq.reshape(half, 1), place,
      mix_pre.reshape(1, D_MODEL), w_in, w_pool.astype(_BF16), pool_scale.reshape(1, POOL_WIDTH),
      g_pool.reshape(1, POOL_WIDTH), g_attn.reshape(1, ATTN_WIDTH), w_out, mix_post.reshape(1, D_MODEL))
    return out.reshape(B, S, D_MODEL)


def kernel(x, positions, ffn1_pre, ffn1_w_gu, ffn1_w_down, ffn1_post, mix_pre, w_in, w_pool, pool_scale, sinks, g_pool, g_attn, w_out, mix_post, ffn2_pre, ffn2_w_gu, ffn2_w_down, ffn2_post):
    B, S, D = x.shape
    for l in range(ffn1_pre.shape[0]):
        x, (w_in_b, w_out_b, w_gu2_b, w_down2_b) = _ffn(
            x.reshape(B * S, D), ffn1_pre[l], ffn1_w_gu[l].astype(_BF16), ffn1_w_down[l].astype(_BF16),
            ffn1_post[l], later_weights=((w_in, l), (w_out, l), (ffn2_w_gu, l), (ffn2_w_down, l)))
        x = _mixer(x.reshape(B, S, D), positions, mix_pre[l], w_in_b, w_pool[l], pool_scale[l], sinks[l],
                   g_pool[l], g_attn[l], w_out_b, mix_post[l])
        x, _ = _ffn(x.reshape(B * S, D), ffn2_pre[l], w_gu2_b, w_down2_b, ffn2_post[l])
        x = x.reshape(B, S, D)
    return x
```

```python
import collections
import functools

import jax
import jax.numpy as jnp
from jax import lax
from jax.experimental import pallas as pl
from jax.experimental.pallas import tpu as pltpu

D_MODEL = 1024
D_FF = 2816
POOL_WINDOWS = (2, 4, 8, 16)
POOL_WIDTH = 512
POOL_GROUP = 128
HEAD_DIM = 64
N_HEADS = 8
N_KV_HEADS = 2
ATTN_WIDTH = 512
KV_WIDTH = 128
BLOCK = 128
ROPE_THETA = 500000.0
ROT_DIM = 16
MIX_WIDTH = 1024
IN_WIDTH = 1280
EPS = 1e-6
NEG_INF = -1e30
LOG2E = 1.4426950408889634

LANES = 128
BF16_SUBLANES = 16
POOL_HALO = 16

FFN_TOKENS = 1024
FFN_CHUNK = 256
MIX_TOKENS = 512
FFN_VMEM_BYTES = 56 * 1024 * 1024
MIX_VMEM_BYTES = 56 * 1024 * 1024

_F32 = jnp.float32
_BF16 = jnp.bfloat16


def _rms(x, g):
    return x * lax.rsqrt(jnp.mean(x * x, axis=-1, keepdims=True) + EPS) * g


def _ffn_kernel(n_cast, x_ref, pre_ref, wgu_ref, wd_ref, post_ref, *rest):
    cast_in, o_ref, cast_out = rest[:n_cast], rest[n_cast], rest[n_cast + 1:2 * n_cast + 1]
    xn_ref, act_ref = rest[2 * n_cast + 1:]
    for src, dst in zip(cast_in, cast_out):
        dst[...] = src[...].astype(_BF16)
    x = x_ref[...]
    xn_ref[...] = _rms(x, pre_ref[...]).astype(_BF16)
    for c in range(D_FF // FFN_CHUNK):
        lo = c * FFN_CHUNK
        xn = xn_ref[...]
        g = jnp.dot(xn, wgu_ref[:, lo:lo + FFN_CHUNK], preferred_element_type=_F32)
        u = jnp.dot(xn, wgu_ref[:, D_FF + lo:D_FF + lo + FFN_CHUNK], preferred_element_type=_F32)
        act_ref[:, lo:lo + FFN_CHUNK] = (g * jax.nn.sigmoid(g) * u).astype(_BF16)
    h = jnp.dot(act_ref[...], wd_ref[...], preferred_element_type=_F32)
    o_ref[...] = x + _rms(h, post_ref[...])


def _ffn(x2d, pre, w_gu, w_down, post, later_weights=()):
    n_tok = x2d.shape[0]
    step_tokens = FFN_TOKENS
    n_steps = n_tok // step_tokens
    const = lambda i: (0, 0)
    resident = dict(pipeline_mode=pl.Buffered(1))
    slab_in, slab_out, slab_shapes = [], [], []
    for w, layer in later_weights:
        _, n_rows, n_cols = w.shape
        slab = next(r for r in range(BF16_SUBLANES, n_rows + 1, BF16_SUBLANES)
                    if n_rows % r == 0 and n_rows // r <= n_steps)
        last = n_rows // slab - 1
        slab_in.append(pl.BlockSpec((None, slab, n_cols), lambda i, layer=layer, last=last:
                                    (layer, jnp.minimum(i, last), 0)))
        slab_out.append(pl.BlockSpec((slab, n_cols), lambda i, last=last: (jnp.minimum(i, last), 0)))
        slab_shapes.append(jax.ShapeDtypeStruct((n_rows, n_cols), _BF16))
    outs = pl.pallas_call(
        functools.partial(_ffn_kernel, len(later_weights)),
        name="ffn",
        grid=(n_steps,),
        in_specs=[
            pl.BlockSpec((step_tokens, D_MODEL), lambda i: (i, 0)),
            pl.BlockSpec((1, D_MODEL), const, **resident),
            pl.BlockSpec((D_MODEL, 2 * D_FF), const, **resident),
            pl.BlockSpec((D_FF, D_MODEL), const, **resident),
            pl.BlockSpec((1, D_MODEL), const, **resident),
        ] + slab_in,
        out_specs=[pl.BlockSpec((step_tokens, D_MODEL), lambda i: (i, 0))] + slab_out,
        out_shape=[jax.ShapeDtypeStruct(x2d.shape, _F32)] + slab_shapes,
        scratch_shapes=[pltpu.VMEM((FFN_TOKENS, D_MODEL), _BF16), pltpu.VMEM((FFN_TOKENS, D_FF), _BF16)],
        compiler_params=pltpu.CompilerParams(
            dimension_semantics=("arbitrary",), vmem_limit_bytes=FFN_VMEM_BYTES),
    )(x2d, pre.reshape(1, D_MODEL), w_gu, w_down, (0.5 * post).reshape(1, D_MODEL),
      *[w for w, _ in later_weights])
    return outs[0], list(outs[1:])


_TileScratch = collections.namedtuple("_TileScratch", "ubuf kbuf vbuf qbuf attn sbuf pbuf ibuf")
_Params = collections.namedtuple(
    "_Params", "sinks invf place pre win wpool pscale gpool gattn wout post band eye hn")


def _tile_scratch_shapes(T):
    return [
        pltpu.VMEM((T + POOL_HALO, POOL_WIDTH), _F32),
        pltpu.VMEM((4, T + BLOCK, LANES), _BF16),
        pltpu.VMEM((4, T + BLOCK, LANES), _BF16),
        pltpu.VMEM((ATTN_WIDTH // LANES, T, LANES), _BF16),
        pltpu.VMEM((ATTN_WIDTH // LANES, T, LANES), _F32),
        pltpu.VMEM((2, 2, 2 * BLOCK, 2 * BLOCK), _F32),
        pltpu.VMEM((2, 2, 2 * BLOCK, 2 * BLOCK), _BF16),
        pltpu.VMEM((2, 2, 2 * BLOCK, LANES), _F32),
    ]


def _rope(t, cos_t, sin_t, lo_lanes):
    partner = jnp.where(lo_lanes, pltpu.roll(t, LANES - ROT_DIM // 2, 1), pltpu.roll(t, ROT_DIM // 2, 1))
    return t * cos_t + partner * sin_t


def _interleave(a, b):
    out, ia, ib = [], 0, 0
    while ia < len(a) or ib < len(b):
        if ib >= len(b) or (ia < len(a) and ia * len(b) <= ib * len(a)):
            out.append(a[ia])
            ia += 1
        else:
            out.append(b[ib])
            ib += 1
    return out


def _run(tasks):
    for task in tasks:
        task()


def _project_tasks(load_x, load_pos, prm, cur, prev, is_start):
    T = MIX_TOKENS
    st = {}
    lane = lax.broadcasted_iota(jnp.int32, (1, LANES), 1)
    dim = lane % HEAD_DIM
    lo_lanes = dim < ROT_DIM // 2

    def win_cols(lo, n):
        return jnp.dot(prm.hn[...], prm.win[:, lo:lo + n], preferred_element_type=_F32)

    def norm_and_carry():
        prm.hn[...] = _rms(load_x(), prm.pre[...]).astype(_BF16)
        if prev is None:
            cur.ubuf[0:POOL_HALO, :] = jnp.zeros((POOL_HALO, POOL_WIDTH), _F32)
            cur.kbuf[:, 0:BLOCK, :] = jnp.zeros((4, BLOCK, LANES), _BF16)
            cur.vbuf[:, 0:BLOCK, :] = jnp.zeros((4, BLOCK, LANES), _BF16)
        else:
            cur.ubuf[0:POOL_HALO, :] = jnp.where(is_start, 0.0, prev.ubuf[T:T + POOL_HALO, :])
            cur.kbuf[:, 0:BLOCK, :] = jnp.where(is_start, 0.0, prev.kbuf[:, T:T + BLOCK, :]).astype(_BF16)
            cur.vbuf[:, 0:BLOCK, :] = jnp.where(is_start, 0.0, prev.vbuf[:, T:T + BLOCK, :]).astype(_BF16)

    def pool_cols(lo):
        cur.ubuf[POOL_HALO:POOL_HALO + T, lo:lo + 2 * LANES] = win_cols(lo, 2 * LANES)

    def rotary_tables():
        ang = load_pos().astype(_F32) * prm.invf[...]
        cs = jnp.concatenate([jnp.cos(ang), jnp.sin(ang)], axis=0)
        cs_hi = cs.astype(_BF16)
        rem = cs - cs_hi.astype(_F32)
        cs_mid = rem.astype(_BF16)
        cs_lo = (rem - cs_mid.astype(_F32)).astype(_BF16)
        cs3 = jnp.concatenate([cs_hi, cs_mid, cs_lo], axis=0)
        tables = lax.dot_general(cs3, prm.place[...], (((0,), (0,)), ((), ())), preferred_element_type=_F32)
        st["cos"] = jnp.where(dim < ROT_DIM, tables[:, :LANES], 1.0)
        st["sin"] = tables[:, LANES:]

    def query_cols(p):
        scale = HEAD_DIM ** -0.5 * LOG2E
        q2 = win_cols(POOL_WIDTH + p * LANES, 2 * LANES)
        for j in range(2):
            qp = q2[:, j * LANES:(j + 1) * LANES]
            cur.qbuf[p + j] = (_rope(qp, st["cos"], st["sin"], lo_lanes) * scale).astype(_BF16)

    def key_value_cols():
        first_half = lane < HEAD_DIM
        kv_cols = win_cols(MIX_WIDTH, 2 * KV_WIDTH)
        k = _rope(kv_cols[:, :KV_WIDTH], st["cos"], st["sin"], lo_lanes)
        v = kv_cols[:, KV_WIDTH:]
        for buf, t in ((cur.kbuf, k), (cur.vbuf, v)):
            t_sw = pltpu.roll(t, HEAD_DIM, 1)
            variants = (jnp.where(first_half, t, 0.0), jnp.where(first_half, 0.0, t_sw),
                        jnp.where(first_half, t_sw, 0.0), jnp.where(first_half, 0.0, t))
            for idx, var in enumerate(variants):
                buf[idx, BLOCK:BLOCK + T, :] = var.astype(_BF16)

    tasks = [norm_and_carry]
    tasks += [functools.partial(pool_cols, lo) for lo in range(0, POOL_WIDTH, 2 * LANES)]
    tasks += [rotary_tables]
    tasks += [functools.partial(query_cols, p) for p in range(0, ATTN_WIDTH // LANES, 2)]
    tasks += [key_value_cols]
    return tasks


def _attend_tasks(prm, cur, s_tile, st):
    T = MIX_TOKENS
    is_start = s_tile == 0
    st["pooled"] = []

    def pool_pair(j):
        row = lax.broadcasted_iota(jnp.int32, (T, 1), 0)
        seen = s_tile * T + row + 1
        diffs = []
        for g in (2 * j, 2 * j + 1):
            w, lo = POOL_WINDOWS[g], g * POOL_GROUP
            win_sum = cur.ubuf[:, lo:lo + POOL_GROUP]
            span = 1
            while span < w:
                win_sum = win_sum + pltpu.roll(win_sum, span, 0)
                span *= 2
            cnt = jnp.minimum(seen, w).astype(_F32)
            d = win_sum[POOL_HALO:, :] / cnt - cur.ubuf[POOL_HALO:POOL_HALO + T, lo:lo + POOL_GROUP]
            diffs.append(d.astype(_BF16))
        st["pooled"].append(jnp.dot(jnp.concatenate(diffs, axis=1), prm.wpool[j], preferred_element_type=_F32))

    upper_rows = lax.broadcasted_iota(jnp.int32, (2 * BLOCK, 1), 0) < BLOCK
    nt = (((1,), (1,)), ((), ()))
    n_units = N_KV_HEADS * T // BLOCK
    ones = jnp.ones((2 * BLOCK, LANES), _BF16)

    def rows(u, n):
        return pl.ds((u // N_KV_HEADS) * BLOCK, n)

    def scores(u):
        kv, slot = u % N_KV_HEADS, u % 2
        q2 = jnp.concatenate([cur.qbuf[2 * kv, rows(u, BLOCK), :], cur.qbuf[2 * kv + 1, rows(u, BLOCK), :]],
                             axis=0)
        q2 = jnp.concatenate([q2, prm.eye[...]], axis=1)
        bias_t = prm.band[jnp.where(is_start, 1, 0) if u < N_KV_HEADS else 0]
        for e in range(2):
            keys = jnp.concatenate([cur.kbuf[2 * kv + e, rows(u, 2 * BLOCK), :], bias_t], axis=1)
            cur.sbuf[slot, e] = lax.dot_general(q2, keys, nt, preferred_element_type=_F32)

    def softmax(u):
        kv, slot = u % N_KV_HEADS, u % 2
        for e in range(2):
            head_a = 4 * kv + e
            sink = jnp.where(upper_rows, prm.sinks[head_a] * LOG2E, prm.sinks[head_a + 2] * LOG2E)
            s = cur.sbuf[slot, e]
            m = jnp.maximum(jnp.max(s, axis=1, keepdims=True), sink)
            cur.pbuf[slot, e] = jnp.exp2(s - m).astype(_BF16)
            cur.ibuf[slot, e] = jnp.broadcast_to(jnp.exp2(sink - m), (2 * BLOCK, LANES))

    def weighted(u):
        kv, slot = u % N_KV_HEADS, u % 2
        out2 = None
        for e in range(2):
            v_ones = jnp.concatenate([cur.vbuf[2 * kv + e, rows(u, 2 * BLOCK), :], ones], axis=1)
            pv = jnp.dot(cur.pbuf[slot, e], v_ones, preferred_element_type=_F32)
            pv = pv[:, :LANES] / (pv[:, LANES:] + cur.ibuf[slot, e])
            out2 = pv if out2 is None else out2 + pv
        cur.attn[2 * kv, rows(u, BLOCK), :] = out2[:BLOCK]
        cur.attn[2 * kv + 1, rows(u, BLOCK), :] = out2[BLOCK:]

    def step(u):
        if 0 <= u - 2 < n_units:
            weighted(u - 2)
        if 0 <= u - 1 < n_units:
            softmax(u - 1)
        if u < n_units:
            scores(u)

    tasks = [functools.partial(pool_pair, j) for j in range(len(POOL_WINDOWS) // 2)]
    tasks += [functools.partial(step, u) for u in range(n_units + 2)]
    return tasks


def _output_tasks(x_ref, o_ref, row0, prm, cur, st):
    T = MIX_TOKENS
    st["hout"] = []

    def group_norms():
        pool_out = jnp.concatenate(st["pooled"], axis=1) * prm.pscale[...]
        y_pool = _rms(pool_out, prm.gpool[...]).astype(_BF16)
        attn = jnp.concatenate([cur.attn[p] for p in range(ATTN_WIDTH // LANES)], axis=1)
        y_attn = _rms(attn, prm.gattn[...]).astype(_BF16)
        st["y"] = jnp.concatenate([y_pool, y_attn], axis=1)

    def out_cols(lo):
        st["hout"].append(jnp.dot(st["y"], prm.wout[:, lo:lo + 2 * LANES], preferred_element_type=_F32))

    def finish():
        hout = jnp.concatenate(st["hout"], axis=1)
        o_ref[row0:row0 + T, :] = x_ref[row0:row0 + T, :] + _rms(hout, prm.post[...])

    return [group_norms] + [functools.partial(out_cols, lo) for lo in range(0, D_MODEL, 2 * LANES)] + [finish]


def _mixer_kernel(tiles_per_seq, sinks_ref, x_ref, xn_ref, pos_ref, posn_ref, invf_ref, place_ref, pre_ref,
                  win_ref, wpool_ref, pscale_ref, gpool_ref, gattn_ref, wout_ref, post_ref, o_ref,
                  band_ref, eye_ref, hn_ref, *scratch):
    T = MIX_TOKENS
    n_set = len(scratch) // 2
    set0, set1 = _TileScratch(*scratch[:n_set]), _TileScratch(*scratch[n_set:])
    prm = _Params(sinks_ref, invf_ref, place_ref, pre_ref, win_ref, wpool_ref, pscale_ref, gpool_ref,
                  gattn_ref, wout_ref, post_ref, band_ref, eye_ref, hn_ref)
    g = pl.program_id(0)
    s_even = (2 * g) % tiles_per_seq

    @pl.when(g == 0)
    def _():
        kj = lax.broadcasted_iota(jnp.int32, (2 * BLOCK, BLOCK), 0)
        qi = lax.broadcasted_iota(jnp.int32, (2 * BLOCK, BLOCK), 1)
        band = (kj > qi) & (kj <= qi + BLOCK)
        band_ref[0] = jnp.where(band, 0.0, NEG_INF).astype(_BF16)
        band_ref[1] = jnp.where(band & (kj >= BLOCK), 0.0, NEG_INF).astype(_BF16)
        eye_ref[...] = jnp.where(kj % BLOCK == qi, 1.0, 0.0).astype(_BF16)
        _run(_project_tasks(lambda: x_ref[0:T, :], lambda: pos_ref[:, 0:T], prm, set0, None, True))

    next_start = (2 * g + 2) % tiles_per_seq == 0
    st_a, st_b = {}, {}
    project_b = _project_tasks(lambda: x_ref[T:2 * T, :], lambda: pos_ref[:, T:2 * T], prm, set1, set0, False)
    project_next = _project_tasks(lambda: xn_ref[...], lambda: posn_ref[...], prm, set0, set1, next_start)
    _run(_interleave(project_b, _attend_tasks(prm, set0, s_even, st_a)))
    _run(_interleave(_output_tasks(x_ref, o_ref, 0, prm, set0, st_a) + project_next,
                     _attend_tasks(prm, set1, s_even + 1, st_b)))
    _run(_output_tasks(x_ref, o_ref, T, prm, set1, st_b))


def _mixer(x, positions, mix_pre, w_in, w_pool, pool_scale, sinks, g_pool, g_attn, w_out, mix_post):
    B, S, _ = x.shape
    T = MIX_TOKENS
    n_tiles = B * S // T
    assert S % (2 * T) == 0
    half = ROT_DIM // 2
    inv_freq = ROPE_THETA ** (-jnp.arange(0, ROT_DIM, 2, dtype=_F32) / ROT_DIM)
    dim = jnp.arange(LANES) % HEAD_DIM
    freq = jnp.arange(half)[:, None]
    cos_place = ((dim[None, :] < ROT_DIM) & (dim[None, :] % half == freq)).astype(_F32)
    sin_place = cos_place * jnp.where(dim < half, -1.0, 1.0)[None, :]
    zeros = jnp.zeros_like(cos_place)
    place = jnp.concatenate([jnp.concatenate([cos_place, zeros], axis=1),
                             jnp.concatenate([zeros, sin_place], axis=1)], axis=0)
    place = jnp.tile(place, (3, 1)).astype(_BF16)
    wp = w_pool.astype(_BF16)
    zero = jnp.zeros((POOL_GROUP, POOL_GROUP), _BF16)
    w_pool_pairs = jnp.stack([jnp.block([[wp[2 * j], zero], [zero, wp[2 * j + 1]]])
                              for j in range(len(POOL_WINDOWS) // 2)])

    resident = dict(pipeline_mode=pl.Buffered(1))
    full = lambda shape: pl.BlockSpec(shape, lambda g, sinks_ref: (0,) * len(shape), **resident)
    next_tile = lambda g, sinks_ref: jnp.minimum(2 * g + 2, n_tiles - 1)
    grid_spec = pltpu.PrefetchScalarGridSpec(
        num_scalar_prefetch=1,
        grid=(n_tiles // 2,),
        in_specs=[
            pl.BlockSpec((2 * T, D_MODEL), lambda g, sinks_ref: (g, 0)),
            pl.BlockSpec((T, D_MODEL), lambda g, sinks_ref: (next_tile(g, sinks_ref), 0)),
            pl.BlockSpec((1, 2 * T), lambda g, sinks_ref: (0, g)),
            pl.BlockSpec((1, T), lambda g, sinks_ref: (0, next_tile(g, sinks_ref))),
            full((half, 1)),
            full((3 * ROT_DIM, 2 * LANES)),
            full((1, D_MODEL)),
            full((D_MODEL, IN_WIDTH)),
            full((len(POOL_WINDOWS) // 2, 2 * POOL_GROUP, 2 * POOL_GROUP)),
            full((1, POOL_WIDTH)),
            full((1, POOL_WIDTH)),
            full((1, ATTN_WIDTH)),
            full((MIX_WIDTH, D_MODEL)),
            full((1, D_MODEL)),
        ],
        out_specs=pl.BlockSpec((2 * T, D_MODEL), lambda g, sinks_ref: (g, 0)),
        scratch_shapes=[pltpu.VMEM((2, 2 * BLOCK, BLOCK), _BF16), pltpu.VMEM((2 * BLOCK, BLOCK), _BF16),
                        pltpu.VMEM((T, D_MODEL), _BF16)]
        + _tile_scratch_shapes(T) + _tile_scratch_shapes(T),
    )
    out = pl.pallas_call(
        functools.partial(_mixer_kernel, S // T),
        name="mixer",
        grid_spec=grid_spec,
        out_shape=jax.ShapeDtypeStruct((B * S, D_MODEL), _F32),
        compiler_params=pltpu.CompilerParams(
            dimension_semantics=("arbitrary",), vmem_limit_bytes=MIX_VMEM_BYTES),
    )(sinks.astype(_F32), x.reshape(B * S, D_MODEL), x.reshape(B * S, D_MODEL),
      positions.reshape(1, B * S), positions.reshape(1, B * S), inv_freq.reshape(half, 1), place,
      mix_pre.reshape(1, D_MODEL), w_in, w_pool_pairs, pool_scale.reshape(1, POOL_WIDTH),
      g_pool.reshape(1, POOL_WIDTH), g_attn.reshape(1, ATTN_WIDTH), w_out, mix_post.reshape(1, D_MODEL))
    return out.reshape(B, S, D_MODEL)


def kernel(x, positions, ffn1_pre, ffn1_w_gu, ffn1_w_down, ffn1_post, mix_pre, w_in, w_pool, pool_scale, sinks, g_pool, g_attn, w_out, mix_post, ffn2_pre, ffn2_w_gu, ffn2_w_down, ffn2_post):
    B, S, D = x.shape
    for l in range(ffn1_pre.shape[0]):
        x, (w_in_b, w_out_b, w_gu2_b, w_down2_b) = _ffn(
            x.reshape(B * S, D), ffn1_pre[l], ffn1_w_gu[l].astype(_BF16), ffn1_w_down[l].astype(_BF16),
            ffn1_post[l], later_weights=((w_in, l), (w_out, l), (ffn2_w_gu, l), (ffn2_w_down, l)))
        x = _mixer(x.reshape(B, S, D), positions, mix_pre[l], w_in_b, w_pool[l], pool_scale[l], sinks[l],
                   g_pool[l], g_attn[l], w_out_b, mix_post[l])
        x, _ = _ffn(x.reshape(B * S, D), ffn2_pre[l], w_gu2_b, w_down2_b, ffn2_post[l])
        x = x.reshape(B, S, D)
    return x
```

```python
import collections
import functools

import jax
import jax.numpy as jnp
from jax import lax
from jax.experimental import pallas as pl
from jax.experimental.pallas import tpu as pltpu

D_MODEL = 1024
D_FF = 2816
POOL_WINDOWS = (2, 4, 8, 16)
POOL_WIDTH = 512
POOL_GROUP = 128
HEAD_DIM = 64
N_HEADS = 8
N_KV_HEADS = 2
ATTN_WIDTH = 512
KV_WIDTH = 128
BLOCK = 128
ROPE_THETA = 500000.0
ROT_DIM = 16
MIX_WIDTH = 1024
IN_WIDTH = 1280
EPS = 1e-6
NEG_INF = -1e30
LOG2E = 1.4426950408889634

LANES = 128
BF16_SUBLANES = 16
POOL_HALO = 16

FFN_TOKENS = 1024
FFN_CHUNK = 256
MIX_TOKENS = 512
FFN_VMEM_BYTES = 56 * 1024 * 1024
MIX_VMEM_BYTES = 56 * 1024 * 1024

_F32 = jnp.float32
_BF16 = jnp.bfloat16


def _rms(x, g):
    return x * lax.rsqrt(jnp.mean(x * x, axis=-1, keepdims=True) + EPS) * g


def _ffn_kernel(n_cast, x_ref, pre_ref, wgu_ref, wd_ref, post_ref, *rest):
    cast_in, o_ref, cast_out = rest[:n_cast], rest[n_cast], rest[n_cast + 1:2 * n_cast + 1]
    xn_ref, act_ref = rest[2 * n_cast + 1:]
    for src, dst in zip(cast_in, cast_out):
        dst[...] = src[...].astype(_BF16)
    x = x_ref[...]
    xn_ref[...] = _rms(x, pre_ref[...]).astype(_BF16)
    for c in range(D_FF // FFN_CHUNK):
        lo = c * FFN_CHUNK
        xn = xn_ref[...]
        g = jnp.dot(xn, wgu_ref[:, lo:lo + FFN_CHUNK], preferred_element_type=_F32)
        u = jnp.dot(xn, wgu_ref[:, D_FF + lo:D_FF + lo + FFN_CHUNK], preferred_element_type=_F32)
        act_ref[:, lo:lo + FFN_CHUNK] = (g * jax.nn.sigmoid(g) * u).astype(_BF16)
    h = jnp.dot(act_ref[...], wd_ref[...], preferred_element_type=_F32)
    o_ref[...] = x + _rms(h, post_ref[...])


def _ffn(x2d, pre, w_gu, w_down, post, later_weights=()):
    n_tok = x2d.shape[0]
    step_tokens = FFN_TOKENS
    n_steps = n_tok // step_tokens
    const = lambda i: (0, 0)
    resident = dict(pipeline_mode=pl.Buffered(1))
    slab_in, slab_out, slab_shapes = [], [], []
    for w, layer in later_weights:
        _, n_rows, n_cols = w.shape
        slab = next(r for r in range(BF16_SUBLANES, n_rows + 1, BF16_SUBLANES)
                    if n_rows % r == 0 and n_rows // r <= n_steps)
        last = n_rows // slab - 1
        slab_in.append(pl.BlockSpec((None, slab, n_cols), lambda i, layer=layer, last=last:
                                    (layer, jnp.minimum(i, last), 0)))
        slab_out.append(pl.BlockSpec((slab, n_cols), lambda i, last=last: (jnp.minimum(i, last), 0)))
        slab_shapes.append(jax.ShapeDtypeStruct((n_rows, n_cols), _BF16))
    outs = pl.pallas_call(
        functools.partial(_ffn_kernel, len(later_weights)),
        name="ffn",
        grid=(n_steps,),
        in_specs=[
            pl.BlockSpec((step_tokens, D_MODEL), lambda i: (i, 0)),
            pl.BlockSpec((1, D_MODEL), const, **resident),
            pl.BlockSpec((D_MODEL, 2 * D_FF), const, **resident),
            pl.BlockSpec((D_FF, D_MODEL), const, **resident),
            pl.BlockSpec((1, D_MODEL), const, **resident),
        ] + slab_in,
        out_specs=[pl.BlockSpec((step_tokens, D_MODEL), lambda i: (i, 0))] + slab_out,
        out_shape=[jax.ShapeDtypeStruct(x2d.shape, _F32)] + slab_shapes,
        scratch_shapes=[pltpu.VMEM((FFN_TOKENS, D_MODEL), _BF16), pltpu.VMEM((FFN_TOKENS, D_FF), _BF16)],
        compiler_params=pltpu.CompilerParams(
            dimension_semantics=("arbitrary",), vmem_limit_bytes=FFN_VMEM_BYTES),
    )(x2d, pre.reshape(1, D_MODEL), w_gu, w_down, (0.5 * post).reshape(1, D_MODEL),
      *[w for w, _ in later_weights])
    return outs[0], list(outs[1:])


_TileScratch = collections.namedtuple("_TileScratch", "ubuf kbuf vbuf qbuf attn sbuf pbuf ibuf")
_Params = collections.namedtuple(
    "_Params", "sinks invf place pre win wpool pscale gpool gattn wout post band eye hn")


def _tile_scratch_shapes(T):
    return [
        pltpu.VMEM((T + POOL_HALO, POOL_WIDTH), _F32),
        pltpu.VMEM((4, T + BLOCK, LANES), _BF16),
        pltpu.VMEM((4, T + BLOCK, LANES), _BF16),
        pltpu.VMEM((ATTN_WIDTH // LANES, T, LANES), _BF16),
        pltpu.VMEM((ATTN_WIDTH // LANES, T, LANES), _F32),
        pltpu.VMEM((2, 2, 2 * BLOCK, 2 * BLOCK), _F32),
        pltpu.VMEM((2, 2, 2 * BLOCK, 2 * BLOCK), _BF16),
        pltpu.VMEM((2, 2, 2 * BLOCK, LANES), _F32),
    ]


def _rope(t, cos_t, sin_t, lo_lanes):
    partner = jnp.where(lo_lanes, pltpu.roll(t, LANES - ROT_DIM // 2, 1), pltpu.roll(t, ROT_DIM // 2, 1))
    return t * cos_t + partner * sin_t


def _interleave(a, b):
    out, ia, ib = [], 0, 0
    while ia < len(a) or ib < len(b):
        if ib >= len(b) or (ia < len(a) and ia * len(b) <= ib * len(a)):
            out.append(a[ia])
            ia += 1
        else:
            out.append(b[ib])
            ib += 1
    return out


def _run(tasks):
    for task in tasks:
        task()


def _project_tasks(load_x, load_pos, prm, cur, prev, is_start):
    T = MIX_TOKENS
    st = {}
    lane = lax.broadcasted_iota(jnp.int32, (1, LANES), 1)
    dim = lane % HEAD_DIM
    lo_lanes = dim < ROT_DIM // 2

    def win_cols(lo, n):
        return jnp.dot(prm.hn[...], prm.win[:, lo:lo + n], preferred_element_type=_F32)

    def norm_and_carry():
        prm.hn[...] = _rms(load_x(), prm.pre[...]).astype(_BF16)
        if prev is None:
            cur.ubuf[0:POOL_HALO, :] = jnp.zeros((POOL_HALO, POOL_WIDTH), _F32)
            cur.kbuf[:, 0:BLOCK, :] = jnp.zeros((4, BLOCK, LANES), _BF16)
            cur.vbuf[:, 0:BLOCK, :] = jnp.zeros((4, BLOCK, LANES), _BF16)
        else:
            cur.ubuf[0:POOL_HALO, :] = jnp.where(is_start, 0.0, prev.ubuf[T:T + POOL_HALO, :])
            cur.kbuf[:, 0:BLOCK, :] = jnp.where(is_start, 0.0, prev.kbuf[:, T:T + BLOCK, :]).astype(_BF16)
            cur.vbuf[:, 0:BLOCK, :] = jnp.where(is_start, 0.0, prev.vbuf[:, T:T + BLOCK, :]).astype(_BF16)

    def pool_cols(lo):
        cur.ubuf[POOL_HALO:POOL_HALO + T, lo:lo + 2 * LANES] = win_cols(lo, 2 * LANES)

    def rotary_tables():
        ang = load_pos().astype(_F32) * prm.invf[...]
        cs = jnp.concatenate([jnp.cos(ang), jnp.sin(ang)], axis=0)
        cs_hi = cs.astype(_BF16)
        rem = cs - cs_hi.astype(_F32)
        cs_mid = rem.astype(_BF16)
        cs_lo = (rem - cs_mid.astype(_F32)).astype(_BF16)
        cs3 = jnp.concatenate([cs_hi, cs_mid, cs_lo], axis=0)
        tables = lax.dot_general(cs3, prm.place[...], (((0,), (0,)), ((), ())), preferred_element_type=_F32)
        st["cos"] = jnp.where(dim < ROT_DIM, tables[:, :LANES], 1.0)
        st["sin"] = tables[:, LANES:]

    def query_cols(p):
        scale = HEAD_DIM ** -0.5 * LOG2E
        q2 = win_cols(POOL_WIDTH + p * LANES, 2 * LANES)
        for j in range(2):
            qp = q2[:, j * LANES:(j + 1) * LANES]
            cur.qbuf[p + j] = (_rope(qp, st["cos"], st["sin"], lo_lanes) * scale).astype(_BF16)

    def key_value_cols():
        first_half = lane < HEAD_DIM
        kv_cols = win_cols(MIX_WIDTH, 2 * KV_WIDTH)
        k = _rope(kv_cols[:, :KV_WIDTH], st["cos"], st["sin"], lo_lanes)
        v = kv_cols[:, KV_WIDTH:]
        for buf, t in ((cur.kbuf, k), (cur.vbuf, v)):
            t_sw = pltpu.roll(t, HEAD_DIM, 1)
            variants = (jnp.where(first_half, t, 0.0), jnp.where(first_half, 0.0, t_sw),
                        jnp.where(first_half, t_sw, 0.0), jnp.where(first_half, 0.0, t))
            for idx, var in enumerate(variants):
                buf[idx, BLOCK:BLOCK + T, :] = var.astype(_BF16)

    tasks = [norm_and_carry]
    tasks += [functools.partial(pool_cols, lo) for lo in range(0, POOL_WIDTH, 2 * LANES)]
    tasks += [rotary_tables]
    tasks += [functools.partial(query_cols, p) for p in range(0, ATTN_WIDTH // LANES, 2)]
    tasks += [key_value_cols]
    return tasks


def _attend_tasks(prm, cur, s_tile, st):
    T = MIX_TOKENS
    is_start = s_tile == 0
    st["pooled"] = []

    def pool_group(g, w):
        lo = g * POOL_GROUP
        win_sum = cur.ubuf[:, lo:lo + POOL_GROUP]
        span = 1
        while span < w:
            win_sum = win_sum + pltpu.roll(win_sum, span, 0)
            span *= 2
        row = lax.broadcasted_iota(jnp.int32, (POOL_HALO, 1), 0)
        cnt = jnp.minimum(s_tile * T + row + 1, w).astype(_F32)
        mean = jnp.concatenate([win_sum[POOL_HALO:2 * POOL_HALO, :] / cnt,
                                win_sum[2 * POOL_HALO:, :] * (1.0 / w)], axis=0)
        d = mean - cur.ubuf[POOL_HALO:POOL_HALO + T, lo:lo + POOL_GROUP]
        st["pooled"].append(jnp.dot(d.astype(_BF16), prm.wpool[g], preferred_element_type=_F32))

    upper_rows = lax.broadcasted_iota(jnp.int32, (2 * BLOCK, 1), 0) < BLOCK
    nt = (((1,), (1,)), ((), ()))
    n_units = N_KV_HEADS * T // BLOCK
    ones = jnp.ones((2 * BLOCK, LANES), _BF16)

    def rows(u, n):
        return pl.ds((u // N_KV_HEADS) * BLOCK, n)

    def scores(u):
        kv, slot = u % N_KV_HEADS, u % 2
        q2 = jnp.concatenate([cur.qbuf[2 * kv, rows(u, BLOCK), :], cur.qbuf[2 * kv + 1, rows(u, BLOCK), :]],
                             axis=0)
        q2 = jnp.concatenate([q2, prm.eye[...]], axis=1)
        bias_t = prm.band[jnp.where(is_start, 1, 0) if u < N_KV_HEADS else 0]
        for e in range(2):
            keys = jnp.concatenate([cur.kbuf[2 * kv + e, rows(u, 2 * BLOCK), :], bias_t], axis=1)
            cur.sbuf[slot, e] = lax.dot_general(q2, keys, nt, preferred_element_type=_F32)

    def softmax(u):
        kv, slot = u % N_KV_HEADS, u % 2
        for e in range(2):
            head_a = 4 * kv + e
            sink = jnp.where(upper_rows, prm.sinks[head_a] * LOG2E, prm.sinks[head_a + 2] * LOG2E)
            s = cur.sbuf[slot, e]
            m = jnp.maximum(jnp.max(s, axis=1, keepdims=True), sink)
            cur.pbuf[slot, e] = jnp.exp2(s - m).astype(_BF16)
            cur.ibuf[slot, e] = jnp.broadcast_to(jnp.exp2(sink - m), (2 * BLOCK, LANES))

    def weighted(u):
        kv, slot = u % N_KV_HEADS, u % 2
        out2 = None
        for e in range(2):
            v_ones = jnp.concatenate([cur.vbuf[2 * kv + e, rows(u, 2 * BLOCK), :], ones], axis=1)
            pv = jnp.dot(cur.pbuf[slot, e], v_ones, preferred_element_type=_F32)
            pv = pv[:, :LANES] / (pv[:, LANES:] + cur.ibuf[slot, e])
            out2 = pv if out2 is None else out2 + pv
        cur.attn[2 * kv, rows(u, BLOCK), :] = out2[:BLOCK]
        cur.attn[2 * kv + 1, rows(u, BLOCK), :] = out2[BLOCK:]

    def step(u):
        if 0 <= u - 2 < n_units:
            weighted(u - 2)
        if 0 <= u - 1 < n_units:
            softmax(u - 1)
        if u < n_units:
            scores(u)

    tasks = [functools.partial(pool_group, g, w) for g, w in enumerate(POOL_WINDOWS)]
    tasks += [functools.partial(step, u) for u in range(n_units + 2)]
    return tasks


def _output_tasks(x_ref, o_ref, row0, prm, cur, st):
    T = MIX_TOKENS
    st["hout"] = []

    def group_norms():
        pool_out = jnp.concatenate(st["pooled"], axis=1) * prm.pscale[...]
        st["y_pool"] = _rms(pool_out, prm.gpool[...]).astype(_BF16)
        attn = jnp.concatenate([cur.attn[p] for p in range(ATTN_WIDTH // LANES)], axis=1)
        st["y_attn"] = _rms(attn, prm.gattn[...]).astype(_BF16)

    def out_cols(lo):
        st["hout"].append(
            jnp.dot(st["y_pool"], prm.wout[0:POOL_WIDTH, lo:lo + 2 * LANES], preferred_element_type=_F32)
            + jnp.dot(st["y_attn"], prm.wout[POOL_WIDTH:, lo:lo + 2 * LANES], preferred_element_type=_F32))

    def finish():
        hout = jnp.concatenate(st["hout"], axis=1)
        o_ref[row0:row0 + T, :] = x_ref[row0:row0 + T, :] + _rms(hout, prm.post[...])

    return [group_norms] + [functools.partial(out_cols, lo) for lo in range(0, D_MODEL, 2 * LANES)] + [finish]


def _mixer_kernel(tiles_per_seq, sinks_ref, x_ref, xn_ref, pos_ref, posn_ref, invf_ref, place_ref, pre_ref,
                  win_ref, wpool_ref, pscale_ref, gpool_ref, gattn_ref, wout_ref, post_ref, o_ref,
                  band_ref, eye_ref, hn_ref, *scratch):
    T = MIX_TOKENS
    n_set = len(scratch) // 2
    set0, set1 = _TileScratch(*scratch[:n_set]), _TileScratch(*scratch[n_set:])
    prm = _Params(sinks_ref, invf_ref, place_ref, pre_ref, win_ref, wpool_ref, pscale_ref, gpool_ref,
                  gattn_ref, wout_ref, post_ref, band_ref, eye_ref, hn_ref)
    g = pl.program_id(0)
    s_even = (2 * g) % tiles_per_seq

    @pl.when(g == 0)
    def _():
        kj = lax.broadcasted_iota(jnp.int32, (2 * BLOCK, BLOCK), 0)
        qi = lax.broadcasted_iota(jnp.int32, (2 * BLOCK, BLOCK), 1)
        band = (kj > qi) & (kj <= qi + BLOCK)
        band_ref[0] = jnp.where(band, 0.0, NEG_INF).astype(_BF16)
        band_ref[1] = jnp.where(band & (kj >= BLOCK), 0.0, NEG_INF).astype(_BF16)
        eye_ref[...] = jnp.where(kj % BLOCK == qi, 1.0, 0.0).astype(_BF16)
        _run(_project_tasks(lambda: x_ref[0:T, :], lambda: pos_ref[:, 0:T], prm, set0, None, True))

    next_start = (2 * g + 2) % tiles_per_seq == 0
    st_a, st_b = {}, {}
    project_b = _project_tasks(lambda: x_ref[T:2 * T, :], lambda: pos_ref[:, T:2 * T], prm, set1, set0, False)
    project_next = _project_tasks(lambda: xn_ref[...], lambda: posn_ref[...], prm, set0, set1, next_start)
    _run(_interleave(project_b, _attend_tasks(prm, set0, s_even, st_a)))
    _run(_interleave(_output_tasks(x_ref, o_ref, 0, prm, set0, st_a) + project_next,
                     _attend_tasks(prm, set1, s_even + 1, st_b)))
    _run(_output_tasks(x_ref, o_ref, T, prm, set1, st_b))


def _mixer(x, positions, mix_pre, w_in, w_pool, pool_scale, sinks, g_pool, g_attn, w_out, mix_post):
    B, S, _ = x.shape
    T = MIX_TOKENS
    n_tiles = B * S // T
    assert S % (2 * T) == 0
    half = ROT_DIM // 2
    inv_freq = ROPE_THETA ** (-jnp.arange(0, ROT_DIM, 2, dtype=_F32) / ROT_DIM)
    dim = jnp.arange(LANES) % HEAD_DIM
    freq = jnp.arange(half)[:, None]
    cos_place = ((dim[None, :] < ROT_DIM) & (dim[None, :] % half == freq)).astype(_F32)
    sin_place = cos_place * jnp.where(dim < half, -1.0, 1.0)[None, :]
    zeros = jnp.zeros_like(cos_place)
    place = jnp.concatenate([jnp.concatenate([cos_place, zeros], axis=1),
                             jnp.concatenate([zeros, sin_place], axis=1)], axis=0)
    place = jnp.tile(place, (3, 1)).astype(_BF16)

    resident = dict(pipeline_mode=pl.Buffered(1))
    full = lambda shape: pl.BlockSpec(shape, lambda g, sinks_ref: (0,) * len(shape), **resident)
    next_tile = lambda g, sinks_ref: jnp.minimum(2 * g + 2, n_tiles - 1)
    grid_spec = pltpu.PrefetchScalarGridSpec(
        num_scalar_prefetch=1,
        grid=(n_tiles // 2,),
        in_specs=[
            pl.BlockSpec((2 * T, D_MODEL), lambda g, sinks_ref: (g, 0)),
            pl.BlockSpec((T, D_MODEL), lambda g, sinks_ref: (next_tile(g, sinks_ref), 0)),
            pl.BlockSpec((1, 2 * T), lambda g, sinks_ref: (0, g)),
            pl.BlockSpec((1, T), lambda g, sinks_ref: (0, next_tile(g, sinks_ref))),
            full((half, 1)),
            full((3 * ROT_DIM, 2 * LANES)),
            full((1, D_MODEL)),
            full((D_MODEL, IN_WIDTH)),
            full((len(POOL_WINDOWS), POOL_GROUP, POOL_GROUP)),
            full((1, POOL_WIDTH)),
            full((1, POOL_WIDTH)),
            full((1, ATTN_WIDTH)),
            full((MIX_WIDTH, D_MODEL)),
            full((1, D_MODEL)),
        ],
        out_specs=pl.BlockSpec((2 * T, D_MODEL), lambda g, sinks_ref: (g, 0)),
        scratch_shapes=[pltpu.VMEM((2, 2 * BLOCK, BLOCK), _BF16), pltpu.VMEM((2 * BLOCK, BLOCK), _BF16),
                        pltpu.VMEM((T, D_MODEL), _BF16)]
        + _tile_scratch_shapes(T) + _tile_scratch_shapes(T),
    )
    out = pl.pallas_call(
        functools.partial(_mixer_kernel, S // T),
        name="mixer",
        grid_spec=grid_spec,
        out_shape=jax.ShapeDtypeStruct((B * S, D_MODEL), _F32),
        compiler_params=pltpu.CompilerParams(
            dimension_semantics=("arbitrary",), vmem_limit_bytes=MIX_VMEM_BYTES),
    )(sinks.astype(_F32), x.reshape(B * S, D_MODEL), x.reshape(B * S, D_MODEL),
      positions.reshape(1, B * S), positions.reshape(1, B * S), inv_freq.reshape(half, 1), place,
      mix_pre.reshape(1, D_MODEL), w_in, w_pool.astype(_BF16), pool_scale.reshape(1, POOL_WIDTH),
      g_pool.reshape(1, POOL_WIDTH), g_attn.reshape(1, ATTN_WIDTH), w_out, mix_post.reshape(1, D_MODEL))
    return out.reshape(B, S, D_MODEL)


def kernel(x, positions, ffn1_pre, ffn1_w_gu, ffn1_w_down, ffn1_post, mix_pre, w_in, w_pool, pool_scale, sinks, g_pool, g_attn, w_out, mix_post, ffn2_pre, ffn2_w_gu, ffn2_w_down, ffn2_post):
    B, S, D = x.shape
    for l in range(ffn1_pre.shape[0]):
        x, (w_in_b, w_out_b, w_gu2_b, w_down2_b) = _ffn(
            x.reshape(B * S, D), ffn1_pre[l], ffn1_w_gu[l].astype(_BF16), ffn1_w_down[l].astype(_BF16),
            ffn1_post[l], later_weights=((w_in, l), (w_out, l), (ffn2_w_gu, l), (ffn2_w_down, l)))
        x = _mixer(x.reshape(B, S, D), positions, mix_pre[l], w_in_b, w_pool[l], pool_scale[l], sinks[l],
                   g_pool[l], g_attn[l], w_out_b, mix_post[l])
        x, _ = _ffn(x.reshape(B * S, D), ffn2_pre[l], w_gu2_b, w_down2_b, ffn2_post[l])
        x = x.reshape(B, S, D)
    return x
```

```python
import collections
import functools

import jax
import jax.numpy as jnp
from jax import lax
from jax.experimental import pallas as pl
from jax.experimental.pallas import tpu as pltpu

D_MODEL = 1024
D_FF = 2816
POOL_WINDOWS = (2, 4, 8, 16)
POOL_WIDTH = 512
POOL_GROUP = 128
HEAD_DIM = 64
N_HEADS = 8
N_KV_HEADS = 2
ATTN_WIDTH = 512
KV_WIDTH = 128
BLOCK = 128
ROPE_THETA = 500000.0
ROT_DIM = 16
MIX_WIDTH = 1024
IN_WIDTH = 1280
EPS = 1e-6
NEG_INF = -1e30
LOG2E = 1.4426950408889634

LANES = 128
BF16_SUBLANES = 16
POOL_HALO = 16

FFN_TOKENS = 1024
FFN_CHUNK = 256
MIX_TOKENS = 512
FFN_VMEM_BYTES = 56 * 1024 * 1024
MIX_VMEM_BYTES = 56 * 1024 * 1024

_F32 = jnp.float32
_BF16 = jnp.bfloat16


def _rms_unit(x):
    return x * lax.rsqrt(jnp.mean(x * x, axis=-1, keepdims=True) + EPS)


def _rms(x, g):
    return _rms_unit(x) * g


def _ffn_kernel(n_cast, x_ref, wgu_ref, wd_ref, post_ref, *rest):
    cast_in, o_ref, cast_out = rest[:2 * n_cast], rest[2 * n_cast], rest[2 * n_cast + 1:3 * n_cast + 1]
    xn_ref, act_ref = rest[3 * n_cast + 1:]
    for src, gain, dst in zip(cast_in[0::2], cast_in[1::2], cast_out):
        dst[...] = (src[...] * gain[...]).astype(_BF16)
    x = x_ref[...]
    xn_ref[...] = _rms_unit(x).astype(_BF16)
    for c in range(D_FF // FFN_CHUNK):
        lo = c * FFN_CHUNK
        xn = xn_ref[...]
        g = jnp.dot(xn, wgu_ref[:, lo:lo + FFN_CHUNK], preferred_element_type=_F32)
        u = jnp.dot(xn, wgu_ref[:, D_FF + lo:D_FF + lo + FFN_CHUNK], preferred_element_type=_F32)
        act_ref[:, lo:lo + FFN_CHUNK] = (g * jax.nn.sigmoid(g) * u).astype(_BF16)
    h = jnp.dot(act_ref[...], wd_ref[...], preferred_element_type=_F32)
    o_ref[...] = x + _rms(h, post_ref[...])


def _ffn(x2d, w_gu, w_down, post, later_weights=()):
    n_tok = x2d.shape[0]
    step_tokens = FFN_TOKENS
    n_steps = n_tok // step_tokens
    const = lambda i: (0, 0)
    resident = dict(pipeline_mode=pl.Buffered(1))
    slab_in, slab_out, slab_shapes, slab_args = [], [], [], []
    for w, layer, gain in later_weights:
        _, n_rows, n_cols = w.shape
        slab = next(r for r in range(BF16_SUBLANES, n_rows + 1, BF16_SUBLANES)
                    if n_rows % r == 0 and n_rows // r <= n_steps)
        last = n_rows // slab - 1
        slab_in.append(pl.BlockSpec((None, slab, n_cols), lambda i, layer=layer, last=last:
                                    (layer, jnp.minimum(i, last), 0)))
        slab_in.append(pl.BlockSpec((slab, 1), lambda i, last=last: (jnp.minimum(i, last), 0)))
        slab_out.append(pl.BlockSpec((slab, n_cols), lambda i, last=last: (jnp.minimum(i, last), 0)))
        slab_shapes.append(jax.ShapeDtypeStruct((n_rows, n_cols), _BF16))
        slab_args += [w, gain.astype(_F32).reshape(n_rows, 1)]
    outs = pl.pallas_call(
        functools.partial(_ffn_kernel, len(later_weights)),
        name="ffn",
        grid=(n_steps,),
        in_specs=[
            pl.BlockSpec((step_tokens, D_MODEL), lambda i: (i, 0)),
            pl.BlockSpec((D_MODEL, 2 * D_FF), const, **resident),
            pl.BlockSpec((D_FF, D_MODEL), const, **resident),
            pl.BlockSpec((1, D_MODEL), const, **resident),
        ] + slab_in,
        out_specs=[pl.BlockSpec((step_tokens, D_MODEL), lambda i: (i, 0))] + slab_out,
        out_shape=[jax.ShapeDtypeStruct(x2d.shape, _F32)] + slab_shapes,
        scratch_shapes=[pltpu.VMEM((FFN_TOKENS, D_MODEL), _BF16), pltpu.VMEM((FFN_TOKENS, D_FF), _BF16)],
        compiler_params=pltpu.CompilerParams(
            dimension_semantics=("arbitrary",), vmem_limit_bytes=FFN_VMEM_BYTES),
    )(x2d, w_gu, w_down, (0.5 * post).reshape(1, D_MODEL), *slab_args)
    return outs[0], list(outs[1:])


_TileScratch = collections.namedtuple("_TileScratch", "ubuf kbuf vbuf qbuf attn sbuf pbuf ibuf")
_Params = collections.namedtuple(
    "_Params", "sinks invf place win wpool wout post band eye hn")


def _tile_scratch_shapes(T):
    return [
        pltpu.VMEM((T + POOL_HALO, POOL_WIDTH), _F32),
        pltpu.VMEM((4, T + BLOCK, LANES), _BF16),
        pltpu.VMEM((4, T + BLOCK, LANES), _BF16),
        pltpu.VMEM((ATTN_WIDTH // LANES, T, LANES), _BF16),
        pltpu.VMEM((ATTN_WIDTH // LANES, T, LANES), _F32),
        pltpu.VMEM((2, 2, 2 * BLOCK, 2 * BLOCK), _F32),
        pltpu.VMEM((2, 2, 2 * BLOCK, 2 * BLOCK), _BF16),
        pltpu.VMEM((2, 2, 2 * BLOCK, LANES), _F32),
    ]


def _rope(t, cos_t, sin_t, lo_lanes):
    partner = jnp.where(lo_lanes, pltpu.roll(t, LANES - ROT_DIM // 2, 1), pltpu.roll(t, ROT_DIM // 2, 1))
    return t * cos_t + partner * sin_t


def _interleave(a, b):
    out, ia, ib = [], 0, 0
    while ia < len(a) or ib < len(b):
        if ib >= len(b) or (ia < len(a) and ia * len(b) <= ib * len(a)):
            out.append(a[ia])
            ia += 1
        else:
            out.append(b[ib])
            ib += 1
    return out


def _run(tasks):
    for task in tasks:
        task()


def _project_tasks(load_x, load_pos, prm, cur, prev, is_start):
    T = MIX_TOKENS
    st = {}
    lane = lax.broadcasted_iota(jnp.int32, (1, LANES), 1)
    dim = lane % HEAD_DIM
    lo_lanes = dim < ROT_DIM // 2

    def win_cols(lo, n):
        return jnp.dot(prm.hn[...], prm.win[:, lo:lo + n], preferred_element_type=_F32)

    def norm_and_carry():
        prm.hn[...] = _rms_unit(load_x()).astype(_BF16)
        if prev is None:
            cur.ubuf[0:POOL_HALO, :] = jnp.zeros((POOL_HALO, POOL_WIDTH), _F32)
            cur.kbuf[:, 0:BLOCK, :] = jnp.zeros((4, BLOCK, LANES), _BF16)
            cur.vbuf[:, 0:BLOCK, :] = jnp.zeros((4, BLOCK, LANES), _BF16)
        else:
            cur.ubuf[0:POOL_HALO, :] = jnp.where(is_start, 0.0, prev.ubuf[T:T + POOL_HALO, :])
            cur.kbuf[:, 0:BLOCK, :] = jnp.where(is_start, 0.0, prev.kbuf[:, T:T + BLOCK, :]).astype(_BF16)
            cur.vbuf[:, 0:BLOCK, :] = jnp.where(is_start, 0.0, prev.vbuf[:, T:T + BLOCK, :]).astype(_BF16)

    def pool_cols(lo):
        cur.ubuf[POOL_HALO:POOL_HALO + T, lo:lo + 2 * LANES] = win_cols(lo, 2 * LANES)

    def rotary_tables():
        ang = load_pos().astype(_F32) * prm.invf[...]
        cs = jnp.concatenate([jnp.cos(ang), jnp.sin(ang)], axis=0)
        cs_hi = cs.astype(_BF16)
        rem = cs - cs_hi.astype(_F32)
        cs_mid = rem.astype(_BF16)
        cs_lo = (rem - cs_mid.astype(_F32)).astype(_BF16)
        cs3 = jnp.concatenate([cs_hi, cs_mid, cs_lo], axis=0)
        tables = lax.dot_general(cs3, prm.place[...], (((0,), (0,)), ((), ())), preferred_element_type=_F32)
        st["cos"] = jnp.where(dim < ROT_DIM, tables[:, :LANES], 1.0)
        st["sin"] = tables[:, LANES:]

    def query_cols(p):
        scale = HEAD_DIM ** -0.5 * LOG2E
        q2 = win_cols(POOL_WIDTH + p * LANES, 2 * LANES)
        for j in range(2):
            qp = q2[:, j * LANES:(j + 1) * LANES]
            cur.qbuf[p + j] = (_rope(qp, st["cos"], st["sin"], lo_lanes) * scale).astype(_BF16)

    def key_value_cols():
        first_half = lane < HEAD_DIM
        kv_cols = win_cols(MIX_WIDTH, 2 * KV_WIDTH)
        k = _rope(kv_cols[:, :KV_WIDTH], st["cos"], st["sin"], lo_lanes)
        v = kv_cols[:, KV_WIDTH:]
        for buf, t in ((cur.kbuf, k), (cur.vbuf, v)):
            t_sw = pltpu.roll(t, HEAD_DIM, 1)
            variants = (jnp.where(first_half, t, 0.0), jnp.where(first_half, 0.0, t_sw),
                        jnp.where(first_half, t_sw, 0.0), jnp.where(first_half, 0.0, t))
            for idx, var in enumerate(variants):
                buf[idx, BLOCK:BLOCK + T, :] = var.astype(_BF16)

    tasks = [norm_and_carry]
    tasks += [functools.partial(pool_cols, lo) for lo in range(0, POOL_WIDTH, 2 * LANES)]
    tasks += [rotary_tables]
    tasks += [functools.partial(query_cols, p) for p in range(0, ATTN_WIDTH // LANES, 2)]
    tasks += [key_value_cols]
    return tasks


def _attend_tasks(prm, cur, s_tile, st):
    T = MIX_TOKENS
    is_start = s_tile == 0
    st["pooled"] = []

    def pool_group(g, w):
        lo = g * POOL_GROUP
        win_sum = cur.ubuf[:, lo:lo + POOL_GROUP]
        span = 1
        while span < w:
            win_sum = win_sum + pltpu.roll(win_sum, span, 0)
            span *= 2
        row = lax.broadcasted_iota(jnp.int32, (POOL_HALO, 1), 0)
        cnt = jnp.minimum(s_tile * T + row + 1, w).astype(_F32)
        mean = jnp.concatenate([win_sum[POOL_HALO:2 * POOL_HALO, :] / cnt,
                                win_sum[2 * POOL_HALO:, :] * (1.0 / w)], axis=0)
        d = mean - cur.ubuf[POOL_HALO:POOL_HALO + T, lo:lo + POOL_GROUP]
        st["pooled"].append(jnp.dot(d.astype(_BF16), prm.wpool[g], preferred_element_type=_F32))

    upper_rows = lax.broadcasted_iota(jnp.int32, (2 * BLOCK, 1), 0) < BLOCK
    nt = (((1,), (1,)), ((), ()))
    n_units = N_KV_HEADS * T // BLOCK
    ones = jnp.ones((2 * BLOCK, LANES), _BF16)

    def rows(u, n):
        return pl.ds((u // N_KV_HEADS) * BLOCK, n)

    def scores(u):
        kv, slot = u % N_KV_HEADS, u % 2
        q2 = jnp.concatenate([cur.qbuf[2 * kv, rows(u, BLOCK), :], cur.qbuf[2 * kv + 1, rows(u, BLOCK), :]],
                             axis=0)
        q2 = jnp.concatenate([q2, prm.eye[...]], axis=1)
        bias_t = prm.band[jnp.where(is_start, 1, 0) if u < N_KV_HEADS else 0]
        for e in range(2):
            keys = jnp.concatenate([cur.kbuf[2 * kv + e, rows(u, 2 * BLOCK), :], bias_t], axis=1)
            cur.sbuf[slot, e] = lax.dot_general(q2, keys, nt, preferred_element_type=_F32)

    def softmax(u):
        kv, slot = u % N_KV_HEADS, u % 2
        for e in range(2):
            head_a = 4 * kv + e
            sink = jnp.where(upper_rows, prm.sinks[head_a] * LOG2E, prm.sinks[head_a + 2] * LOG2E)
            s = cur.sbuf[slot, e]
            m = jnp.maximum(jnp.max(s, axis=1, keepdims=True), sink)
            cur.pbuf[slot, e] = jnp.exp2(s - m).astype(_BF16)
            cur.ibuf[slot, e] = jnp.broadcast_to(jnp.exp2(sink - m), (2 * BLOCK, LANES))

    def weighted(u):
        kv, slot = u % N_KV_HEADS, u % 2
        out2 = None
        for e in range(2):
            v_ones = jnp.concatenate([cur.vbuf[2 * kv + e, rows(u, 2 * BLOCK), :], ones], axis=1)
            pv = jnp.dot(cur.pbuf[slot, e], v_ones, preferred_element_type=_F32)
            pv = pv[:, :LANES] / (pv[:, LANES:] + cur.ibuf[slot, e])
            out2 = pv if out2 is None else out2 + pv
        cur.attn[2 * kv, rows(u, BLOCK), :] = out2[:BLOCK]
        cur.attn[2 * kv + 1, rows(u, BLOCK), :] = out2[BLOCK:]

    def step(u):
        if 0 <= u - 2 < n_units:
            weighted(u - 2)
        if 0 <= u - 1 < n_units:
            softmax(u - 1)
        if u < n_units:
            scores(u)

    tasks = [functools.partial(pool_group, g, w) for g, w in enumerate(POOL_WINDOWS)]
    tasks += [functools.partial(step, u) for u in range(n_units + 2)]
    return tasks


def _output_tasks(x_ref, o_ref, row0, prm, cur, st):
    T = MIX_TOKENS
    st["hout"] = []

    def group_norms():
        pool_out = jnp.concatenate(st["pooled"], axis=1)
        st["y_pool"] = _rms_unit(pool_out).astype(_BF16)
        attn = jnp.concatenate([cur.attn[p] for p in range(ATTN_WIDTH // LANES)], axis=1)
        st["y_attn"] = _rms_unit(attn).astype(_BF16)

    def out_cols(lo):
        st["hout"].append(
            jnp.dot(st["y_pool"], prm.wout[0:POOL_WIDTH, lo:lo + 2 * LANES], preferred_element_type=_F32)
            + jnp.dot(st["y_attn"], prm.wout[POOL_WIDTH:, lo:lo + 2 * LANES], preferred_element_type=_F32))

    def finish():
        hout = jnp.concatenate(st["hout"], axis=1)
        o_ref[row0:row0 + T, :] = x_ref[row0:row0 + T, :] + _rms(hout, prm.post[...])

    return [group_norms] + [functools.partial(out_cols, lo) for lo in range(0, D_MODEL, 2 * LANES)] + [finish]


def _mixer_kernel(tiles_per_seq, sinks_ref, x_ref, xn_ref, pos_ref, posn_ref, invf_ref, place_ref,
                  win_ref, wpool_ref, wout_ref, post_ref, o_ref,
                  band_ref, eye_ref, hn_ref, *scratch):
    T = MIX_TOKENS
    n_set = len(scratch) // 2
    set0, set1 = _TileScratch(*scratch[:n_set]), _TileScratch(*scratch[n_set:])
    prm = _Params(sinks_ref, invf_ref, place_ref, win_ref, wpool_ref, wout_ref, post_ref, band_ref, eye_ref,
                  hn_ref)
    g = pl.program_id(0)
    s_even = (2 * g) % tiles_per_seq

    @pl.when(g == 0)
    def _():
        kj = lax.broadcasted_iota(jnp.int32, (2 * BLOCK, BLOCK), 0)
        qi = lax.broadcasted_iota(jnp.int32, (2 * BLOCK, BLOCK), 1)
        band = (kj > qi) & (kj <= qi + BLOCK)
        band_ref[0] = jnp.where(band, 0.0, NEG_INF).astype(_BF16)
        band_ref[1] = jnp.where(band & (kj >= BLOCK), 0.0, NEG_INF).astype(_BF16)
        eye_ref[...] = jnp.where(kj % BLOCK == qi, 1.0, 0.0).astype(_BF16)
        _run(_project_tasks(lambda: x_ref[0:T, :], lambda: pos_ref[:, 0:T], prm, set0, None, True))

    next_start = (2 * g + 2) % tiles_per_seq == 0
    st_a, st_b = {}, {}
    project_b = _project_tasks(lambda: x_ref[T:2 * T, :], lambda: pos_ref[:, T:2 * T], prm, set1, set0, False)
    project_next = _project_tasks(lambda: xn_ref[...], lambda: posn_ref[...], prm, set0, set1, next_start)
    _run(_interleave(project_b, _attend_tasks(prm, set0, s_even, st_a)))
    _run(_interleave(_output_tasks(x_ref, o_ref, 0, prm, set0, st_a) + project_next,
                     _attend_tasks(prm, set1, s_even + 1, st_b)))
    _run(_output_tasks(x_ref, o_ref, T, prm, set1, st_b))


def _mixer(x, positions, w_in, w_pool, sinks, w_out, mix_post):
    B, S, _ = x.shape
    T = MIX_TOKENS
    n_tiles = B * S // T
    assert S % (2 * T) == 0
    half = ROT_DIM // 2
    inv_freq = ROPE_THETA ** (-jnp.arange(0, ROT_DIM, 2, dtype=_F32) / ROT_DIM)
    dim = jnp.arange(LANES) % HEAD_DIM
    freq = jnp.arange(half)[:, None]
    cos_place = ((dim[None, :] < ROT_DIM) & (dim[None, :] % half == freq)).astype(_F32)
    sin_place = cos_place * jnp.where(dim < half, -1.0, 1.0)[None, :]
    zeros = jnp.zeros_like(cos_place)
    place = jnp.concatenate([jnp.concatenate([cos_place, zeros], axis=1),
                             jnp.concatenate([zeros, sin_place], axis=1)], axis=0)
    place = jnp.tile(place, (3, 1)).astype(_BF16)

    resident = dict(pipeline_mode=pl.Buffered(1))
    full = lambda shape: pl.BlockSpec(shape, lambda g, sinks_ref: (0,) * len(shape), **resident)
    next_tile = lambda g, sinks_ref: jnp.minimum(2 * g + 2, n_tiles - 1)
    grid_spec = pltpu.PrefetchScalarGridSpec(
        num_scalar_prefetch=1,
        grid=(n_tiles // 2,),
        in_specs=[
            pl.BlockSpec((2 * T, D_MODEL), lambda g, sinks_ref: (g, 0)),
            pl.BlockSpec((T, D_MODEL), lambda g, sinks_ref: (next_tile(g, sinks_ref), 0)),
            pl.BlockSpec((1, 2 * T), lambda g, sinks_ref: (0, g)),
            pl.BlockSpec((1, T), lambda g, sinks_ref: (0, next_tile(g, sinks_ref))),
            full((half, 1)),
            full((3 * ROT_DIM, 2 * LANES)),
            full((D_MODEL, IN_WIDTH)),
            full((len(POOL_WINDOWS), POOL_GROUP, POOL_GROUP)),
            full((MIX_WIDTH, D_MODEL)),
            full((1, D_MODEL)),
        ],
        out_specs=pl.BlockSpec((2 * T, D_MODEL), lambda g, sinks_ref: (g, 0)),
        scratch_shapes=[pltpu.VMEM((2, 2 * BLOCK, BLOCK), _BF16), pltpu.VMEM((2 * BLOCK, BLOCK), _BF16),
                        pltpu.VMEM((T, D_MODEL), _BF16)]
        + _tile_scratch_shapes(T) + _tile_scratch_shapes(T),
    )
    out = pl.pallas_call(
        functools.partial(_mixer_kernel, S // T),
        name="mixer",
        grid_spec=grid_spec,
        out_shape=jax.ShapeDtypeStruct((B * S, D_MODEL), _F32),
        compiler_params=pltpu.CompilerParams(
            dimension_semantics=("arbitrary",), vmem_limit_bytes=MIX_VMEM_BYTES),
    )(sinks.astype(_F32), x.reshape(B * S, D_MODEL), x.reshape(B * S, D_MODEL),
      positions.reshape(1, B * S), positions.reshape(1, B * S), inv_freq.reshape(half, 1), place,
      w_in, w_pool, w_out, mix_post.reshape(1, D_MODEL))
    return out.reshape(B, S, D_MODEL)


def kernel(x, positions, ffn1_pre, ffn1_w_gu, ffn1_w_down, ffn1_post, mix_pre, w_in, w_pool, pool_scale, sinks, g_pool, g_attn, w_out, mix_post, ffn2_pre, ffn2_w_gu, ffn2_w_down, ffn2_post):
    B, S, D = x.shape
    for l in range(ffn1_pre.shape[0]):
        w_gu1 = (ffn1_w_gu[l] * ffn1_pre[l][:, None]).astype(_BF16)
        w_pool_b = (w_pool[l] * pool_scale[l].reshape(len(POOL_WINDOWS), 1, POOL_GROUP)).astype(_BF16)
        x, (w_in_b, w_out_b, w_gu2_b, w_down2_b) = _ffn(
            x.reshape(B * S, D), w_gu1, ffn1_w_down[l].astype(_BF16), ffn1_post[l],
            later_weights=((w_in, l, mix_pre[l]), (w_out, l, jnp.concatenate([g_pool[l], g_attn[l]])),
                           (ffn2_w_gu, l, ffn2_pre[l]), (ffn2_w_down, l, jnp.ones((D_FF,), _F32))))
        x = _mixer(x.reshape(B, S, D), positions, w_in_b, w_pool_b, sinks[l], w_out_b, mix_post[l])
        x, _ = _ffn(x.reshape(B * S, D), w_gu2_b, w_down2_b, ffn2_post[l])
        x = x.reshape(B, S, D)
    return x
```

```python
import collections
import functools

import jax
import jax.numpy as jnp
from jax import lax
from jax.experimental import pallas as pl
from jax.experimental.pallas import tpu as pltpu

D_MODEL = 1024
D_FF = 2816
POOL_WINDOWS = (2, 4, 8, 16)
POOL_WIDTH = 512
POOL_GROUP = 128
HEAD_DIM = 64
N_HEADS = 8
N_KV_HEADS = 2
ATTN_WIDTH = 512
KV_WIDTH = 128
BLOCK = 128
ROPE_THETA = 500000.0
ROT_DIM = 16
MIX_WIDTH = 1024
IN_WIDTH = 1280
EPS = 1e-6
NEG_INF = -1e30
LOG2E = 1.4426950408889634

LANES = 128
BF16_SUBLANES = 16
POOL_HALO = 16

FFN_TOKENS = 1024
FFN_CHUNK = 256
MIX_TOKENS = 512
FFN_VMEM_BYTES = 60 * 1024 * 1024
MIX_VMEM_BYTES = 56 * 1024 * 1024

_F32 = jnp.float32
_BF16 = jnp.bfloat16


def _rms_unit(x):
    return x * lax.rsqrt(jnp.mean(x * x, axis=-1, keepdims=True) + EPS)


def _rms(x, g):
    return _rms_unit(x) * g


def _ffn_kernel(n_cast, x_ref, xnext_ref, wgu_ref, wd_ref, post_ref, *rest):
    cast_in, o_ref, cast_out = rest[:2 * n_cast], rest[2 * n_cast], rest[2 * n_cast + 1:3 * n_cast + 1]
    xn_ref, act_ref = rest[3 * n_cast + 1:]
    for src, gain, dst in zip(cast_in[0::2], cast_in[1::2], cast_out):
        dst[...] = (src[...] * gain[...]).astype(_BF16)

    def gate_up(c):
        lo = c * FFN_CHUNK
        xn = xn_ref[...]
        g = jnp.dot(xn, wgu_ref[:, lo:lo + FFN_CHUNK], preferred_element_type=_F32)
        u = jnp.dot(xn, wgu_ref[:, D_FF + lo:D_FF + lo + FFN_CHUNK], preferred_element_type=_F32)
        act_ref[:, lo:lo + FFN_CHUNK] = (g * jax.nn.sigmoid(g) * u).astype(_BF16)

    @pl.when(pl.program_id(0) == 0)
    def _():
        xn_ref[...] = _rms_unit(x_ref[...]).astype(_BF16)
        gate_up(0)

    for c in range(1, D_FF // FFN_CHUNK):
        gate_up(c)
    h = jnp.dot(act_ref[...], wd_ref[...], preferred_element_type=_F32)
    xn_ref[...] = _rms_unit(xnext_ref[...]).astype(_BF16)
    gate_up(0)
    o_ref[...] = x_ref[...] + _rms(h, post_ref[...])


def _ffn(x2d, w_gu, w_down, post, later_weights=()):
    n_tok = x2d.shape[0]
    step_tokens = FFN_TOKENS
    n_steps = n_tok // step_tokens
    const = lambda i: (0, 0)
    resident = dict(pipeline_mode=pl.Buffered(1))
    slab_in, slab_out, slab_shapes, slab_args = [], [], [], []
    for w, layer, gain in later_weights:
        _, n_rows, n_cols = w.shape
        slab = next(r for r in range(BF16_SUBLANES, n_rows + 1, BF16_SUBLANES)
                    if n_rows % r == 0 and n_rows // r <= n_steps)
        last = n_rows // slab - 1
        slab_in.append(pl.BlockSpec((None, slab, n_cols), lambda i, layer=layer, last=last:
                                    (layer, jnp.minimum(i, last), 0)))
        slab_in.append(pl.BlockSpec((slab, 1), lambda i, last=last: (jnp.minimum(i, last), 0)))
        slab_out.append(pl.BlockSpec((slab, n_cols), lambda i, last=last: (jnp.minimum(i, last), 0)))
        slab_shapes.append(jax.ShapeDtypeStruct((n_rows, n_cols), _BF16))
        slab_args += [w, gain.astype(_F32).reshape(n_rows, 1)]
    outs = pl.pallas_call(
        functools.partial(_ffn_kernel, len(later_weights)),
        name="ffn",
        grid=(n_steps,),
        in_specs=[
            pl.BlockSpec((step_tokens, D_MODEL), lambda i: (i, 0)),
            pl.BlockSpec((step_tokens, D_MODEL), lambda i: (jnp.minimum(i + 1, n_steps - 1), 0)),
            pl.BlockSpec((D_MODEL, 2 * D_FF), const, **resident),
            pl.BlockSpec((D_FF, D_MODEL), const, **resident),
            pl.BlockSpec((1, D_MODEL), const, **resident),
        ] + slab_in,
        out_specs=[pl.BlockSpec((step_tokens, D_MODEL), lambda i: (i, 0))] + slab_out,
        out_shape=[jax.ShapeDtypeStruct(x2d.shape, _F32)] + slab_shapes,
        scratch_shapes=[pltpu.VMEM((FFN_TOKENS, D_MODEL), _BF16), pltpu.VMEM((FFN_TOKENS, D_FF), _BF16)],
        compiler_params=pltpu.CompilerParams(
            dimension_semantics=("arbitrary",), vmem_limit_bytes=FFN_VMEM_BYTES),
    )(x2d, x2d, w_gu, w_down, (0.5 * post).reshape(1, D_MODEL), *slab_args)
    return outs[0], list(outs[1:])


_TileScratch = collections.namedtuple("_TileScratch", "ubuf kbuf vbuf qbuf attn sbuf pbuf ibuf")
_Params = collections.namedtuple(
    "_Params", "sinks invf place win wpool wout post band eye hn")


def _tile_scratch_shapes(T):
    return [
        pltpu.VMEM((T + POOL_HALO, POOL_WIDTH), _F32),
        pltpu.VMEM((4, T + BLOCK, LANES), _BF16),
        pltpu.VMEM((4, T + BLOCK, LANES), _BF16),
        pltpu.VMEM((ATTN_WIDTH // LANES, T, LANES), _BF16),
        pltpu.VMEM((ATTN_WIDTH // LANES, T, LANES), _F32),
        pltpu.VMEM((2, 2, 2 * BLOCK, 2 * BLOCK), _F32),
        pltpu.VMEM((2, 2, 2 * BLOCK, 2 * BLOCK), _BF16),
        pltpu.VMEM((2, 2, 2 * BLOCK, LANES), _F32),
    ]


def _rope(t, cos_t, sin_t, lo_lanes):
    partner = jnp.where(lo_lanes, pltpu.roll(t, LANES - ROT_DIM // 2, 1), pltpu.roll(t, ROT_DIM // 2, 1))
    return t * cos_t + partner * sin_t


def _interleave(a, b):
    out, ia, ib = [], 0, 0
    while ia < len(a) or ib < len(b):
        if ib >= len(b) or (ia < len(a) and ia * len(b) <= ib * len(a)):
            out.append(a[ia])
            ia += 1
        else:
            out.append(b[ib])
            ib += 1
    return out


def _run(tasks):
    for task in tasks:
        task()


def _project_tasks(load_x, load_pos, prm, cur, prev, is_start):
    T = MIX_TOKENS
    st = {}
    lane = lax.broadcasted_iota(jnp.int32, (1, LANES), 1)
    dim = lane % HEAD_DIM
    lo_lanes = dim < ROT_DIM // 2

    def win_cols(lo, n):
        return jnp.dot(prm.hn[...], prm.win[:, lo:lo + n], preferred_element_type=_F32)

    def norm_and_carry():
        prm.hn[...] = _rms_unit(load_x()).astype(_BF16)
        if prev is None:
            cur.ubuf[0:POOL_HALO, :] = jnp.zeros((POOL_HALO, POOL_WIDTH), _F32)
            cur.kbuf[:, 0:BLOCK, :] = jnp.zeros((4, BLOCK, LANES), _BF16)
            cur.vbuf[:, 0:BLOCK, :] = jnp.zeros((4, BLOCK, LANES), _BF16)
        else:
            cur.ubuf[0:POOL_HALO, :] = jnp.where(is_start, 0.0, prev.ubuf[T:T + POOL_HALO, :])
            cur.kbuf[:, 0:BLOCK, :] = jnp.where(is_start, 0.0, prev.kbuf[:, T:T + BLOCK, :]).astype(_BF16)
            cur.vbuf[:, 0:BLOCK, :] = jnp.where(is_start, 0.0, prev.vbuf[:, T:T + BLOCK, :]).astype(_BF16)

    def pool_cols(lo):
        cur.ubuf[POOL_HALO:POOL_HALO + T, lo:lo + 2 * LANES] = win_cols(lo, 2 * LANES)

    def rotary_tables():
        ang = load_pos().astype(_F32) * prm.invf[...]
        cs = jnp.concatenate([jnp.cos(ang), jnp.sin(ang)], axis=0)
        cs_hi = cs.astype(_BF16)
        rem = cs - cs_hi.astype(_F32)
        cs_mid = rem.astype(_BF16)
        cs_lo = (rem - cs_mid.astype(_F32)).astype(_BF16)
        cs3 = jnp.concatenate([cs_hi, cs_mid, cs_lo], axis=0)
        tables = lax.dot_general(cs3, prm.place[...], (((0,), (0,)), ((), ())), preferred_element_type=_F32)
        st["cos"] = jnp.where(dim < ROT_DIM, tables[:, :LANES], 1.0)
        st["sin"] = tables[:, LANES:]

    def query_cols(p):
        scale = HEAD_DIM ** -0.5 * LOG2E
        q2 = win_cols(POOL_WIDTH + p * LANES, 2 * LANES)
        for j in range(2):
            qp = q2[:, j * LANES:(j + 1) * LANES]
            cur.qbuf[p + j] = (_rope(qp, st["cos"], st["sin"], lo_lanes) * scale).astype(_BF16)

    def key_value_cols():
        first_half = lane < HEAD_DIM
        kv_cols = win_cols(MIX_WIDTH, 2 * KV_WIDTH)
        k = _rope(kv_cols[:, :KV_WIDTH], st["cos"], st["sin"], lo_lanes)
        v = kv_cols[:, KV_WIDTH:]
        for buf, t in ((cur.kbuf, k), (cur.vbuf, v)):
            t_sw = pltpu.roll(t, HEAD_DIM, 1)
            variants = (jnp.where(first_half, t, 0.0), jnp.where(first_half, 0.0, t_sw),
                        jnp.where(first_half, t_sw, 0.0), jnp.where(first_half, 0.0, t))
            for idx, var in enumerate(variants):
                buf[idx, BLOCK:BLOCK + T, :] = var.astype(_BF16)

    tasks = [norm_and_carry]
    tasks += [functools.partial(pool_cols, lo) for lo in range(0, POOL_WIDTH, 2 * LANES)]
    tasks += [rotary_tables]
    tasks += [functools.partial(query_cols, p) for p in range(0, ATTN_WIDTH // LANES, 2)]
    tasks += [key_value_cols]
    return tasks


def _attend_tasks(prm, cur, s_tile, st):
    T = MIX_TOKENS
    is_start = s_tile == 0
    st["pooled"] = []

    def pool_group(g, w):
        lo = g * POOL_GROUP
        win_sum = cur.ubuf[:, lo:lo + POOL_GROUP]
        span = 1
        while span < w:
            win_sum = win_sum + pltpu.roll(win_sum, span, 0)
            span *= 2
        row = lax.broadcasted_iota(jnp.int32, (POOL_HALO, 1), 0)
        cnt = jnp.minimum(s_tile * T + row + 1, w).astype(_F32)
        mean = jnp.concatenate([win_sum[POOL_HALO:2 * POOL_HALO, :] / cnt,
                                win_sum[2 * POOL_HALO:, :] * (1.0 / w)], axis=0)
        d = mean - cur.ubuf[POOL_HALO:POOL_HALO + T, lo:lo + POOL_GROUP]
        st["pooled"].append(jnp.dot(d.astype(_BF16), prm.wpool[g], preferred_element_type=_F32))

    upper_rows = lax.broadcasted_iota(jnp.int32, (2 * BLOCK, 1), 0) < BLOCK
    nt = (((1,), (1,)), ((), ()))
    n_units = N_KV_HEADS * T // BLOCK
    ones = jnp.ones((2 * BLOCK, LANES), _BF16)

    def rows(u, n):
        return pl.ds((u // N_KV_HEADS) * BLOCK, n)

    def scores(u):
        kv, slot = u % N_KV_HEADS, u % 2
        q2 = jnp.concatenate([cur.qbuf[2 * kv, rows(u, BLOCK), :], cur.qbuf[2 * kv + 1, rows(u, BLOCK), :]],
                             axis=0)
        q2 = jnp.concatenate([q2, prm.eye[...]], axis=1)
        bias_t = prm.band[jnp.where(is_start, 1, 0) if u < N_KV_HEADS else 0]
        for e in range(2):
            keys = jnp.concatenate([cur.kbuf[2 * kv + e, rows(u, 2 * BLOCK), :], bias_t], axis=1)
            cur.sbuf[slot, e] = lax.dot_general(q2, keys, nt, preferred_element_type=_F32)

    def softmax(u):
        kv, slot = u % N_KV_HEADS, u % 2
        for e in range(2):
            head_a = 4 * kv + e
            sink = jnp.where(upper_rows, prm.sinks[head_a] * LOG2E, prm.sinks[head_a + 2] * LOG2E)
            s = cur.sbuf[slot, e]
            m = jnp.maximum(jnp.max(s, axis=1, keepdims=True), sink)
            cur.pbuf[slot, e] = jnp.exp2(s - m).astype(_BF16)
            cur.ibuf[slot, e] = jnp.broadcast_to(jnp.exp2(sink - m), (2 * BLOCK, LANES))

    def weighted(u):
        kv, slot = u % N_KV_HEADS, u % 2
        out2 = None
        for e in range(2):
            v_ones = jnp.concatenate([cur.vbuf[2 * kv + e, rows(u, 2 * BLOCK), :], ones], axis=1)
            pv = jnp.dot(cur.pbuf[slot, e], v_ones, preferred_element_type=_F32)
            pv = pv[:, :LANES] / (pv[:, LANES:] + cur.ibuf[slot, e])
            out2 = pv if out2 is None else out2 + pv
        cur.attn[2 * kv, rows(u, BLOCK), :] = out2[:BLOCK]
        cur.attn[2 * kv + 1, rows(u, BLOCK), :] = out2[BLOCK:]

    def step(u):
        if 0 <= u - 2 < n_units:
            weighted(u - 2)
        if 0 <= u - 1 < n_units:
            softmax(u - 1)
        if u < n_units:
            scores(u)

    tasks = [functools.partial(pool_group, g, w) for g, w in enumerate(POOL_WINDOWS)]
    tasks += [functools.partial(step, u) for u in range(n_units + 2)]
    return tasks


def _output_tasks(x_ref, o_ref, row0, prm, cur, st):
    T = MIX_TOKENS
    st["hout"] = []

    def group_norms():
        pool_out = jnp.concatenate(st["pooled"], axis=1)
        st["y_pool"] = _rms_unit(pool_out).astype(_BF16)
        attn = jnp.concatenate([cur.attn[p] for p in range(ATTN_WIDTH // LANES)], axis=1)
        st["y_attn"] = _rms_unit(attn).astype(_BF16)

    def out_cols(lo):
        st["hout"].append(
            jnp.dot(st["y_pool"], prm.wout[0:POOL_WIDTH, lo:lo + 2 * LANES], preferred_element_type=_F32)
            + jnp.dot(st["y_attn"], prm.wout[POOL_WIDTH:, lo:lo + 2 * LANES], preferred_element_type=_F32))

    def finish():
        hout = jnp.concatenate(st["hout"], axis=1)
        o_ref[row0:row0 + T, :] = x_ref[row0:row0 + T, :] + _rms(hout, prm.post[...])

    return [group_norms] + [functools.partial(out_cols, lo) for lo in range(0, D_MODEL, 2 * LANES)] + [finish]


def _mixer_kernel(tiles_per_seq, sinks_ref, x_ref, xn_ref, pos_ref, posn_ref, invf_ref, place_ref,
                  win_ref, wpool_ref, wout_ref, post_ref, o_ref,
                  band_ref, eye_ref, hn_ref, *scratch):
    T = MIX_TOKENS
    n_set = len(scratch) // 2
    set0, set1 = _TileScratch(*scratch[:n_set]), _TileScratch(*scratch[n_set:])
    prm = _Params(sinks_ref, invf_ref, place_ref, win_ref, wpool_ref, wout_ref, post_ref, band_ref, eye_ref,
                  hn_ref)
    g = pl.program_id(0)
    s_even = (2 * g) % tiles_per_seq

    @pl.when(g == 0)
    def _():
        kj = lax.broadcasted_iota(jnp.int32, (2 * BLOCK, BLOCK), 0)
        qi = lax.broadcasted_iota(jnp.int32, (2 * BLOCK, BLOCK), 1)
        band = (kj > qi) & (kj <= qi + BLOCK)
        band_ref[0] = jnp.where(band, 0.0, NEG_INF).astype(_BF16)
        band_ref[1] = jnp.where(band & (kj >= BLOCK), 0.0, NEG_INF).astype(_BF16)
        eye_ref[...] = jnp.where(kj % BLOCK == qi, 1.0, 0.0).astype(_BF16)
        _run(_project_tasks(lambda: x_ref[0:T, :], lambda: pos_ref[:, 0:T], prm, set0, None, True))

    next_start = (2 * g + 2) % tiles_per_seq == 0
    st_a, st_b = {}, {}
    project_b = _project_tasks(lambda: x_ref[T:2 * T, :], lambda: pos_ref[:, T:2 * T], prm, set1, set0, False)
    project_next = _project_tasks(lambda: xn_ref[...], lambda: posn_ref[...], prm, set0, set1, next_start)
    _run(_interleave(project_b, _attend_tasks(prm, set0, s_even, st_a)))
    _run(_interleave(_output_tasks(x_ref, o_ref, 0, prm, set0, st_a) + project_next,
                     _attend_tasks(prm, set1, s_even + 1, st_b)))
    _run(_output_tasks(x_ref, o_ref, T, prm, set1, st_b))


def _mixer(x, positions, w_in, w_pool, sinks, w_out, mix_post):
    B, S, _ = x.shape
    T = MIX_TOKENS
    n_tiles = B * S // T
    assert S % (2 * T) == 0
    half = ROT_DIM // 2
    inv_freq = ROPE_THETA ** (-jnp.arange(0, ROT_DIM, 2, dtype=_F32) / ROT_DIM)
    dim = jnp.arange(LANES) % HEAD_DIM
    freq = jnp.arange(half)[:, None]
    cos_place = ((dim[None, :] < ROT_DIM) & (dim[None, :] % half == freq)).astype(_F32)
    sin_place = cos_place * jnp.where(dim < half, -1.0, 1.0)[None, :]
    zeros = jnp.zeros_like(cos_place)
    place = jnp.concatenate([jnp.concatenate([cos_place, zeros], axis=1),
                             jnp.concatenate([zeros, sin_place], axis=1)], axis=0)
    place = jnp.tile(place, (3, 1)).astype(_BF16)

    resident = dict(pipeline_mode=pl.Buffered(1))
    full = lambda shape: pl.BlockSpec(shape, lambda g, sinks_ref: (0,) * len(shape), **resident)
    next_tile = lambda g, sinks_ref: jnp.minimum(2 * g + 2, n_tiles - 1)
    grid_spec = pltpu.PrefetchScalarGridSpec(
        num_scalar_prefetch=1,
        grid=(n_tiles // 2,),
        in_specs=[
            pl.BlockSpec((2 * T, D_MODEL), lambda g, sinks_ref: (g, 0)),
            pl.BlockSpec((T, D_MODEL), lambda g, sinks_ref: (next_tile(g, sinks_ref), 0)),
            pl.BlockSpec((1, 2 * T), lambda g, sinks_ref: (0, g)),
            pl.BlockSpec((1, T), lambda g, sinks_ref: (0, next_tile(g, sinks_ref))),
            full((half, 1)),
            full((3 * ROT_DIM, 2 * LANES)),
            full((D_MODEL, IN_WIDTH)),
            full((len(POOL_WINDOWS), POOL_GROUP, POOL_GROUP)),
            full((MIX_WIDTH, D_MODEL)),
            full((1, D_MODEL)),
        ],
        out_specs=pl.BlockSpec((2 * T, D_MODEL), lambda g, sinks_ref: (g, 0)),
        scratch_shapes=[pltpu.VMEM((2, 2 * BLOCK, BLOCK), _BF16), pltpu.VMEM((2 * BLOCK, BLOCK), _BF16),
                        pltpu.VMEM((T, D_MODEL), _BF16)]
        + _tile_scratch_shapes(T) + _tile_scratch_shapes(T),
    )
    out = pl.pallas_call(
        functools.partial(_mixer_kernel, S // T),
        name="mixer",
        grid_spec=grid_spec,
        out_shape=jax.ShapeDtypeStruct((B * S, D_MODEL), _F32),
        compiler_params=pltpu.CompilerParams(
            dimension_semantics=("arbitrary",), vmem_limit_bytes=MIX_VMEM_BYTES),
    )(sinks.astype(_F32), x.reshape(B * S, D_MODEL), x.reshape(B * S, D_MODEL),
      positions.reshape(1, B * S), positions.reshape(1, B * S), inv_freq.reshape(half, 1), place,
      w_in, w_pool, w_out, mix_post.reshape(1, D_MODEL))
    return out.reshape(B, S, D_MODEL)


def kernel(x, positions, ffn1_pre, ffn1_w_gu, ffn1_w_down, ffn1_post, mix_pre, w_in, w_pool, pool_scale, sinks, g_pool, g_attn, w_out, mix_post, ffn2_pre, ffn2_w_gu, ffn2_w_down, ffn2_post):
    B, S, D = x.shape
    for l in range(ffn1_pre.shape[0]):
        w_gu1 = (ffn1_w_gu[l] * ffn1_pre[l][:, None]).astype(_BF16)
        w_pool_b = (w_pool[l] * pool_scale[l].reshape(len(POOL_WINDOWS), 1, POOL_GROUP)).astype(_BF16)
        x, (w_in_b, w_out_b, w_gu2_b, w_down2_b) = _ffn(
            x.reshape(B * S, D), w_gu1, ffn1_w_down[l].astype(_BF16), ffn1_post[l],
            later_weights=((w_in, l, mix_pre[l]), (w_out, l, jnp.concatenate([g_pool[l], g_attn[l]])),
                           (ffn2_w_gu, l, ffn2_pre[l]), (ffn2_w_down, l, jnp.ones((D_FF,), _F32))))
        x = _mixer(x.reshape(B, S, D), positions, w_in_b, w_pool_b, sinks[l], w_out_b, mix_post[l])
        x, _ = _ffn(x.reshape(B * S, D), w_gu2_b, w_down2_b, ffn2_post[l])
        x = x.reshape(B, S, D)
    return x
```

```python
import collections
import functools

import jax
import jax.numpy as jnp
from jax import lax
from jax.experimental import pallas as pl
from jax.experimental.pallas import tpu as pltpu

D_MODEL = 1024
D_FF = 2816
POOL_WINDOWS = (2, 4, 8, 16)
POOL_WIDTH = 512
POOL_GROUP = 128
HEAD_DIM = 64
N_HEADS = 8
N_KV_HEADS = 2
ATTN_WIDTH = 512
KV_WIDTH = 128
BLOCK = 128
ROPE_THETA = 500000.0
ROT_DIM = 16
MIX_WIDTH = 1024
IN_WIDTH = 1280
EPS = 1e-6
NEG_INF = -1e30
LOG2E = 1.4426950408889634

LANES = 128
BF16_SUBLANES = 16
POOL_HALO = 16

FFN_TOKENS = 1024
FFN_CHUNK = 256
MIX_TOKENS = 512
FFN_VMEM_BYTES = 56 * 1024 * 1024
MIX_VMEM_BYTES = 56 * 1024 * 1024

_F32 = jnp.float32
_BF16 = jnp.bfloat16


def _rms_unit(x):
    return x * lax.rsqrt(jnp.mean(x * x, axis=-1, keepdims=True) + EPS)


def _rms(x, g):
    return _rms_unit(x) * g


def _ffn_kernel(n_cast, x_ref, wgu_ref, wd_ref, post_ref, *rest):
    cast_in, o_ref, cast_out = rest[:2 * n_cast], rest[2 * n_cast], rest[2 * n_cast + 1:3 * n_cast + 1]
    xn_ref, act_ref = rest[3 * n_cast + 1:]
    for src, gain, dst in zip(cast_in[0::2], cast_in[1::2], cast_out):
        dst[...] = (src[...] * gain[...]).astype(_BF16)
    x = x_ref[...]
    xn_ref[...] = _rms_unit(x).astype(_BF16)
    for c in range(D_FF // FFN_CHUNK):
        lo = c * FFN_CHUNK
        xn = xn_ref[...]
        g = jnp.dot(xn, wgu_ref[:, lo:lo + FFN_CHUNK], preferred_element_type=_F32)
        u = jnp.dot(xn, wgu_ref[:, D_FF + lo:D_FF + lo + FFN_CHUNK], preferred_element_type=_F32)
        act_ref[:, lo:lo + FFN_CHUNK] = (g * jax.nn.sigmoid(g) * u).astype(_BF16)
    h = jnp.dot(act_ref[...], wd_ref[...], preferred_element_type=_F32)
    o_ref[...] = x + _rms(h, post_ref[...])


def _ffn(x2d, w_gu, w_down, post, later_weights=()):
    n_tok = x2d.shape[0]
    step_tokens = FFN_TOKENS
    n_steps = n_tok // step_tokens
    const = lambda i: (0, 0)
    resident = dict(pipeline_mode=pl.Buffered(1))
    slab_in, slab_out, slab_shapes, slab_args = [], [], [], []
    for w, layer, gain in later_weights:
        _, n_rows, n_cols = w.shape
        slab = next(r for r in range(BF16_SUBLANES, n_rows + 1, BF16_SUBLANES)
                    if n_rows % r == 0 and n_rows // r <= n_steps)
        last = n_rows // slab - 1
        slab_in.append(pl.BlockSpec((None, slab, n_cols), lambda i, layer=layer, last=last:
                                    (layer, jnp.minimum(i, last), 0)))
        slab_in.append(pl.BlockSpec((slab, 1), lambda i, last=last: (jnp.minimum(i, last), 0)))
        slab_out.append(pl.BlockSpec((slab, n_cols), lambda i, last=last: (jnp.minimum(i, last), 0)))
        slab_shapes.append(jax.ShapeDtypeStruct((n_rows, n_cols), _BF16))
        slab_args += [w, gain.astype(_F32).reshape(n_rows, 1)]
    outs = pl.pallas_call(
        functools.partial(_ffn_kernel, len(later_weights)),
        name="ffn",
        grid=(n_steps,),
        in_specs=[
            pl.BlockSpec((step_tokens, D_MODEL), lambda i: (i, 0)),
            pl.BlockSpec((D_MODEL, 2 * D_FF), const, **resident),
            pl.BlockSpec((D_FF, D_MODEL), const, **resident),
            pl.BlockSpec((1, D_MODEL), const, **resident),
        ] + slab_in,
        out_specs=[pl.BlockSpec((step_tokens, D_MODEL), lambda i: (i, 0))] + slab_out,
        out_shape=[jax.ShapeDtypeStruct(x2d.shape, _F32)] + slab_shapes,
        scratch_shapes=[pltpu.VMEM((FFN_TOKENS, D_MODEL), _BF16), pltpu.VMEM((FFN_TOKENS, D_FF), _BF16)],
        compiler_params=pltpu.CompilerParams(
            dimension_semantics=("arbitrary",), vmem_limit_bytes=FFN_VMEM_BYTES),
    )(x2d, w_gu, w_down, (0.5 * post).reshape(1, D_MODEL), *slab_args)
    return outs[0], list(outs[1:])


_TileScratch = collections.namedtuple("_TileScratch", "ubuf kbuf vbuf qbuf attn sbuf pbuf ibuf")
_Params = collections.namedtuple(
    "_Params", "sinks invf place win wpool wout post band eye hn")


def _tile_scratch_shapes(T):
    return [
        pltpu.VMEM((T + POOL_HALO, POOL_WIDTH), _F32),
        pltpu.VMEM((4, T + BLOCK, LANES), _BF16),
        pltpu.VMEM((4, T + BLOCK, LANES), _BF16),
        pltpu.VMEM((ATTN_WIDTH // LANES, T, LANES), _BF16),
        pltpu.VMEM((ATTN_WIDTH // LANES, T, LANES), _F32),
        pltpu.VMEM((2, 2, 2 * BLOCK, 2 * BLOCK), _F32),
        pltpu.VMEM((2, 2, 2 * BLOCK, 2 * BLOCK), _BF16),
        pltpu.VMEM((2, 2, 2 * BLOCK, LANES), _F32),
    ]


def _rope(t, cos_t, sin_t, lo_lanes):
    partner = jnp.where(lo_lanes, pltpu.roll(t, LANES - ROT_DIM // 2, 1), pltpu.roll(t, ROT_DIM // 2, 1))
    return t * cos_t + partner * sin_t


def _interleave(a, b):
    out, ia, ib = [], 0, 0
    while ia < len(a) or ib < len(b):
        if ib >= len(b) or (ia < len(a) and ia * len(b) <= ib * len(a)):
            out.append(a[ia])
            ia += 1
        else:
            out.append(b[ib])
            ib += 1
    return out


def _run(tasks):
    for task in tasks:
        task()


def _project_tasks(load_x, load_pos, prm, cur, prev, is_start):
    T = MIX_TOKENS
    st = {}
    lane = lax.broadcasted_iota(jnp.int32, (1, LANES), 1)
    dim = lane % HEAD_DIM
    lo_lanes = dim < ROT_DIM // 2

    def win_cols(lo, n):
        return jnp.dot(prm.hn[...], prm.win[:, lo:lo + n], preferred_element_type=_F32)

    def norm_and_carry():
        prm.hn[...] = _rms_unit(load_x()).astype(_BF16)
        if prev is None:
            cur.ubuf[0:POOL_HALO, :] = jnp.zeros((POOL_HALO, POOL_WIDTH), _F32)
            cur.kbuf[:, 0:BLOCK, :] = jnp.zeros((4, BLOCK, LANES), _BF16)
            cur.vbuf[:, 0:BLOCK, :] = jnp.zeros((4, BLOCK, LANES), _BF16)
        else:
            cur.ubuf[0:POOL_HALO, :] = jnp.where(is_start, 0.0, prev.ubuf[T:T + POOL_HALO, :])
            cur.kbuf[:, 0:BLOCK, :] = jnp.where(is_start, 0.0, prev.kbuf[:, T:T + BLOCK, :]).astype(_BF16)
            cur.vbuf[:, 0:BLOCK, :] = jnp.where(is_start, 0.0, prev.vbuf[:, T:T + BLOCK, :]).astype(_BF16)

    def pool_cols(lo):
        cur.ubuf[POOL_HALO:POOL_HALO + T, lo:lo + 2 * LANES] = win_cols(lo, 2 * LANES)

    def rotary_tables():
        ang = load_pos().astype(_F32) * prm.invf[...]
        cs = jnp.concatenate([jnp.cos(ang), jnp.sin(ang)], axis=0)
        cs_hi = cs.astype(_BF16)
        rem = cs - cs_hi.astype(_F32)
        cs_mid = rem.astype(_BF16)
        cs_lo = (rem - cs_mid.astype(_F32)).astype(_BF16)
        cs3 = jnp.concatenate([cs_hi, cs_mid, cs_lo], axis=0)
        tables = lax.dot_general(cs3, prm.place[...], (((0,), (0,)), ((), ())), preferred_element_type=_F32)
        st["cos"] = jnp.where(dim < ROT_DIM, tables[:, :LANES], 1.0)
        st["sin"] = tables[:, LANES:]

    def query_cols(p):
        scale = HEAD_DIM ** -0.5 * LOG2E
        q2 = win_cols(POOL_WIDTH + p * LANES, 2 * LANES)
        for j in range(2):
            qp = q2[:, j * LANES:(j + 1) * LANES]
            cur.qbuf[p + j] = (_rope(qp, st["cos"], st["sin"], lo_lanes) * scale).astype(_BF16)

    def key_value_cols():
        first_half = lane < HEAD_DIM
        kv_cols = win_cols(MIX_WIDTH, 2 * KV_WIDTH)
        k = _rope(kv_cols[:, :KV_WIDTH], st["cos"], st["sin"], lo_lanes)
        v = kv_cols[:, KV_WIDTH:]
        for buf, t in ((cur.kbuf, k), (cur.vbuf, v)):
            t_sw = pltpu.roll(t, HEAD_DIM, 1)
            variants = (jnp.where(first_half, t, 0.0), jnp.where(first_half, 0.0, t_sw),
                        jnp.where(first_half, t_sw, 0.0), jnp.where(first_half, 0.0, t))
            for idx, var in enumerate(variants):
                buf[idx, BLOCK:BLOCK + T, :] = var.astype(_BF16)

    tasks = [norm_and_carry]
    tasks += [functools.partial(pool_cols, lo) for lo in range(0, POOL_WIDTH, 2 * LANES)]
    tasks += [rotary_tables]
    tasks += [functools.partial(query_cols, p) for p in range(0, ATTN_WIDTH // LANES, 2)]
    tasks += [key_value_cols]
    return tasks


def _attend_tasks(prm, cur, s_tile, st):
    T = MIX_TOKENS
    is_start = s_tile == 0
    st["pooled"] = []

    def pool_group(g, w):
        lo = g * POOL_GROUP
        win_sum = cur.ubuf[:, lo:lo + POOL_GROUP]
        span = 1
        while span < w:
            win_sum = win_sum + pltpu.roll(win_sum, span, 0)
            span *= 2
        row = lax.broadcasted_iota(jnp.int32, (POOL_HALO, 1), 0)
        cnt = jnp.minimum(s_tile * T + row + 1, w).astype(_F32)
        mean = jnp.concatenate([win_sum[POOL_HALO:2 * POOL_HALO, :] / cnt,
                                win_sum[2 * POOL_HALO:, :] * (1.0 / w)], axis=0)
        d = mean - cur.ubuf[POOL_HALO:POOL_HALO + T, lo:lo + POOL_GROUP]
        st["pooled"].append(jnp.dot(d.astype(_BF16), prm.wpool[g], preferred_element_type=_F32))

    upper_rows = lax.broadcasted_iota(jnp.int32, (2 * BLOCK, 1), 0) < BLOCK
    nt = (((1,), (1,)), ((), ()))
    n_units = N_KV_HEADS * T // BLOCK
    ones = jnp.ones((2 * BLOCK, LANES), _BF16)

    def rows(u, n):
        return pl.ds((u // N_KV_HEADS) * BLOCK, n)

    def scores(u):
        kv, slot = u % N_KV_HEADS, u % 2
        q2 = jnp.concatenate([cur.qbuf[2 * kv, rows(u, BLOCK), :], cur.qbuf[2 * kv + 1, rows(u, BLOCK), :]],
                             axis=0)
        q2 = jnp.concatenate([q2, prm.eye[...]], axis=1)
        bias_t = prm.band[jnp.where(is_start, 1, 0) if u < N_KV_HEADS else 0]
        for e in range(2):
            keys = jnp.concatenate([cur.kbuf[2 * kv + e, rows(u, 2 * BLOCK), :], bias_t], axis=1)
            cur.sbuf[slot, e] = lax.dot_general(q2, keys, nt, preferred_element_type=_F32)

    def softmax(u):
        kv, slot = u % N_KV_HEADS, u % 2
        for e in range(2):
            head_a = 4 * kv + e
            sink = jnp.where(upper_rows, prm.sinks[head_a] * LOG2E, prm.sinks[head_a + 2] * LOG2E)
            s = cur.sbuf[slot, e]
            m = jnp.maximum(jnp.max(s, axis=1, keepdims=True), sink)
            cur.pbuf[slot, e] = jnp.exp2(s - m).astype(_BF16)
            cur.ibuf[slot, e] = jnp.broadcast_to(jnp.exp2(sink - m), (2 * BLOCK, LANES))

    def weighted(u):
        kv, slot = u % N_KV_HEADS, u % 2
        out2 = None
        for e in range(2):
            v_ones = jnp.concatenate([cur.vbuf[2 * kv + e, rows(u, 2 * BLOCK), :], ones], axis=1)
            pv = jnp.dot(cur.pbuf[slot, e], v_ones, preferred_element_type=_F32)
            pv = pv[:, :LANES] / (pv[:, LANES:] + cur.ibuf[slot, e])
            out2 = pv if out2 is None else out2 + pv
        cur.attn[2 * kv, rows(u, BLOCK), :] = out2[:BLOCK]
        cur.attn[2 * kv + 1, rows(u, BLOCK), :] = out2[BLOCK:]

    def step(u):
        if 0 <= u - 2 < n_units:
            weighted(u - 2)
        if 0 <= u - 1 < n_units:
            softmax(u - 1)
        if u < n_units:
            scores(u)

    tasks = [functools.partial(pool_group, g, w) for g, w in enumerate(POOL_WINDOWS)]
    tasks += [functools.partial(step, u) for u in range(n_units + 2)]
    return tasks


def _output_tasks(x_ref, o_ref, row0, prm, cur, st):
    T = MIX_TOKENS
    st["hout"] = []

    def group_norms():
        pool_out = jnp.concatenate(st["pooled"], axis=1)
        y_pool = _rms_unit(pool_out).astype(_BF16)
        attn = jnp.concatenate([cur.attn[p] for p in range(ATTN_WIDTH // LANES)], axis=1)
        y_attn = _rms_unit(attn).astype(_BF16)
        st["y"] = jnp.concatenate([y_pool, y_attn], axis=1)

    def out_cols(lo):
        st["hout"].append(jnp.dot(st["y"], prm.wout[:, lo:lo + 2 * LANES], preferred_element_type=_F32))

    def finish():
        hout = jnp.concatenate(st["hout"], axis=1)
        o_ref[row0:row0 + T, :] = x_ref[row0:row0 + T, :] + _rms(hout, prm.post[...])

    return [group_norms] + [functools.partial(out_cols, lo) for lo in range(0, D_MODEL, 2 * LANES)] + [finish]


def _mixer_kernel(tiles_per_seq, sinks_ref, x_ref, xn_ref, pos_ref, posn_ref, invf_ref, place_ref,
                  win_ref, wpool_ref, wout_ref, post_ref, o_ref,
                  band_ref, eye_ref, hn_ref, *scratch):
    T = MIX_TOKENS
    n_set = len(scratch) // 2
    set0, set1 = _TileScratch(*scratch[:n_set]), _TileScratch(*scratch[n_set:])
    prm = _Params(sinks_ref, invf_ref, place_ref, win_ref, wpool_ref, wout_ref, post_ref, band_ref, eye_ref,
                  hn_ref)
    g = pl.program_id(0)
    s_even = (2 * g) % tiles_per_seq

    @pl.when(g == 0)
    def _():
        kj = lax.broadcasted_iota(jnp.int32, (2 * BLOCK, BLOCK), 0)
        qi = lax.broadcasted_iota(jnp.int32, (2 * BLOCK, BLOCK), 1)
        band = (kj > qi) & (kj <= qi + BLOCK)
        band_ref[0] = jnp.where(band, 0.0, NEG_INF).astype(_BF16)
        band_ref[1] = jnp.where(band & (kj >= BLOCK), 0.0, NEG_INF).astype(_BF16)
        eye_ref[...] = jnp.where(kj % BLOCK == qi, 1.0, 0.0).astype(_BF16)
        _run(_project_tasks(lambda: x_ref[0:T, :], lambda: pos_ref[:, 0:T], prm, set0, None, True))

    next_start = (2 * g + 2) % tiles_per_seq == 0
    st_a, st_b = {}, {}
    project_b = _project_tasks(lambda: x_ref[T:2 * T, :], lambda: pos_ref[:, T:2 * T], prm, set1, set0, False)
    project_next = _project_tasks(lambda: xn_ref[...], lambda: posn_ref[...], prm, set0, set1, next_start)
    _run(_interleave(project_b, _attend_tasks(prm, set0, s_even, st_a)))
    _run(_interleave(_output_tasks(x_ref, o_ref, 0, prm, set0, st_a) + project_next,
                     _attend_tasks(prm, set1, s_even + 1, st_b)))
    _run(_output_tasks(x_ref, o_ref, T, prm, set1, st_b))


def _mixer(x, positions, w_in, w_pool, sinks, w_out, mix_post):
    B, S, _ = x.shape
    T = MIX_TOKENS
    n_tiles = B * S // T
    assert S % (2 * T) == 0
    half = ROT_DIM // 2
    inv_freq = ROPE_THETA ** (-jnp.arange(0, ROT_DIM, 2, dtype=_F32) / ROT_DIM)
    dim = jnp.arange(LANES) % HEAD_DIM
    freq = jnp.arange(half)[:, None]
    cos_place = ((dim[None, :] < ROT_DIM) & (dim[None, :] % half == freq)).astype(_F32)
    sin_place = cos_place * jnp.where(dim < half, -1.0, 1.0)[None, :]
    zeros = jnp.zeros_like(cos_place)
    place = jnp.concatenate([jnp.concatenate([cos_place, zeros], axis=1),
                             jnp.concatenate([zeros, sin_place], axis=1)], axis=0)
    place = jnp.tile(place, (3, 1)).astype(_BF16)

    resident = dict(pipeline_mode=pl.Buffered(1))
    full = lambda shape: pl.BlockSpec(shape, lambda g, sinks_ref: (0,) * len(shape), **resident)
    next_tile = lambda g, sinks_ref: jnp.minimum(2 * g + 2, n_tiles - 1)
    grid_spec = pltpu.PrefetchScalarGridSpec(
        num_scalar_prefetch=1,
        grid=(n_tiles // 2,),
        in_specs=[
            pl.BlockSpec((2 * T, D_MODEL), lambda g, sinks_ref: (g, 0)),
            pl.BlockSpec((T, D_MODEL), lambda g, sinks_ref: (next_tile(g, sinks_ref), 0)),
            pl.BlockSpec((1, 2 * T), lambda g, sinks_ref: (0, g)),
            pl.BlockSpec((1, T), lambda g, sinks_ref: (0, next_tile(g, sinks_ref))),
            full((half, 1)),
            full((3 * ROT_DIM, 2 * LANES)),
            full((D_MODEL, IN_WIDTH)),
            full((len(POOL_WINDOWS), POOL_GROUP, POOL_GROUP)),
            full((MIX_WIDTH, D_MODEL)),
            full((1, D_MODEL)),
        ],
        out_specs=pl.BlockSpec((2 * T, D_MODEL), lambda g, sinks_ref: (g, 0)),
        scratch_shapes=[pltpu.VMEM((2, 2 * BLOCK, BLOCK), _BF16), pltpu.VMEM((2 * BLOCK, BLOCK), _BF16),
                        pltpu.VMEM((T, D_MODEL), _BF16)]
        + _tile_scratch_shapes(T) + _tile_scratch_shapes(T),
    )
    out = pl.pallas_call(
        functools.partial(_mixer_kernel, S // T),
        name="mixer",
        grid_spec=grid_spec,
        out_shape=jax.ShapeDtypeStruct((B * S, D_MODEL), _F32),
        compiler_params=pltpu.CompilerParams(
            dimension_semantics=("arbitrary",), vmem_limit_bytes=MIX_VMEM_BYTES),
    )(sinks.astype(_F32), x.reshape(B * S, D_MODEL), x.reshape(B * S, D_MODEL),
      positions.reshape(1, B * S), positions.reshape(1, B * S), inv_freq.reshape(half, 1), place,
      w_in, w_pool, w_out, mix_post.reshape(1, D_MODEL))
    return out.reshape(B, S, D_MODEL)


def kernel(x, positions, ffn1_pre, ffn1_w_gu, ffn1_w_down, ffn1_post, mix_pre, w_in, w_pool, pool_scale, sinks, g_pool, g_attn, w_out, mix_post, ffn2_pre, ffn2_w_gu, ffn2_w_down, ffn2_post):
    B, S, D = x.shape
    for l in range(ffn1_pre.shape[0]):
        w_gu1 = (ffn1_w_gu[l] * ffn1_pre[l][:, None]).astype(_BF16)
        w_pool_b = (w_pool[l] * pool_scale[l].reshape(len(POOL_WINDOWS), 1, POOL_GROUP)).astype(_BF16)
        x, (w_in_b, w_out_b, w_gu2_b, w_down2_b) = _ffn(
            x.reshape(B * S, D), w_gu1, ffn1_w_down[l].astype(_BF16), ffn1_post[l],
            later_weights=((w_in, l, mix_pre[l]), (w_out, l, jnp.concatenate([g_pool[l], g_attn[l]])),
                           (ffn2_w_gu, l, ffn2_pre[l]), (ffn2_w_down, l, jnp.ones((D_FF,), _F32))))
        x = _mixer(x.reshape(B, S, D), positions, w_in_b, w_pool_b, sinks[l], w_out_b, mix_post[l])
        x, _ = _ffn(x.reshape(B * S, D), w_gu2_b, w_down2_b, ffn2_post[l])
        x = x.reshape(B, S, D)
    return x
```
